```python
import jax, jax.numpy as jnp
from jax import lax
import numpy as np

D_MODEL = 1024
BATCH = 8
SEQ = 4096
DEPTH = 1
DEC_BATCH = 16
DEC_SEQ = 16
PAST_LEN = 2048

CHUNK = 64
N_MEM = 256
H_A = 8
HD_A = 128
H_KV = 2
H_IDX = 16
D_IDX = 64
TOPK_MAX = 256
Q_BLOCK = 128
H_G = 4
DK_G = 128
DV_G = 256
GATE_RANK = 16
GATE_TEMP = 16.0
H_M = 4
HD_M = 256
N_GROUPS = 4
E_PER_GROUP = 8
N_EXPERTS = N_GROUPS * E_PER_GROUP
D_EXPERT = 256
TOP_E = 2
MOE_BLOCK = 128
N_BRANCH = 3
EPS = 1e-6
NEG_INF = -1e30
SPLITS = (H_A * HD_A, H_KV * HD_A, H_KV * HD_A,
          H_IDX * D_IDX, D_IDX, H_IDX,
          H_G * DK_G, H_G * DK_G, H_G * DV_G,
          GATE_RANK, H_G * DV_G,
          H_M * HD_M,
          N_BRANCH * D_MODEL)
D_IN = sum(SPLITS)

kernel_name = 'hybrid_stream_dsa_gla_hmoe'


def rmsnorm(x, g):
    xf = x.astype(jnp.float32)
    y = xf * lax.rsqrt(jnp.mean(xf * xf, axis=-1, keepdims=True) + EPS)
    return (y * g).astype(x.dtype)


def split_cols(p):
    parts = []
    off = 0
    for n in SPLITS:
        parts.append(p[..., off:off + n])
        off += n
    return parts


def dsa_attend(q, qi, wi, k, v, ki, q_pos, k_pos, top_k):
    B, Tq = q.shape[:2]
    rel = jax.nn.relu(jnp.einsum('bthd,bsd->bths', qi, ki) * D_IDX ** -0.5)
    score = jnp.einsum('bth,bths->bts', wi, rel).astype(jnp.float32)
    adm = (k_pos[None, :] // CHUNK) <= (q_pos[:, None] // CHUNK)
    score = jnp.where(adm[None], score, NEG_INF)
    sel_score, sel_idx = lax.top_k(score, top_k)
    valid = sel_score > NEG_INF * 0.5
    k_sel = jax.vmap(lambda kb, ib: kb[ib])(k, sel_idx)
    v_sel = jax.vmap(lambda vb, ib: vb[ib])(v, sel_idx)
    qg = q.reshape(B, Tq, H_KV, H_A // H_KV, HD_A)
    logits = jnp.einsum('btgrd,btkgd->btgrk', qg, k_sel).astype(jnp.float32) * HD_A ** -0.5
    logits = jnp.where(valid[:, :, None, None, :], logits, NEG_INF)
    prob = jax.nn.softmax(logits, axis=-1).astype(v.dtype)
    o = jnp.einsum('btgrk,btkgd->btgrd', prob, v_sel)
    return o.reshape(B, Tq, H_A * HD_A)


def gla_chunked(q, k, v, log_a, s0, chunk):
    B, T, H, DK = q.shape
    DV = v.shape[-1]
    n = T // chunk
    r = lambda t: t.reshape(B, n, chunk, H, t.shape[-1])
    q, k, v = r(q), r(k), r(v)
    b = jnp.cumsum(r(log_a.astype(jnp.float32)), axis=2)
    b_last = b[:, :, -1:]
    qe = q * jnp.exp(b).astype(q.dtype)
    ke = k * jnp.exp(-b).astype(k.dtype)
    kd = k * jnp.exp(b_last - b).astype(k.dtype)
    causal = jnp.tril(jnp.ones((chunk, chunk), dtype=bool))
    att = jnp.where(causal, jnp.einsum('bnihk,bnjhk->bnhij', qe, ke), 0.0).astype(v.dtype)
    o_intra = jnp.einsum('bnhij,bnjhv->bnihv', att, v)
    decay = jnp.exp(b_last[:, :, 0]).astype(s0.dtype)

    def step(S, xs):
        qe_c, kd_c, v_c, dec_c, oi_c = xs
        o = oi_c + jnp.einsum('bihk,bhkv->bihv', qe_c, S)
        S = S * dec_c[..., None] + jnp.einsum('bjhk,bjhv->bhkv', kd_c, v_c)
        return S, o

    xs = (jnp.moveaxis(qe, 1, 0), jnp.moveaxis(kd, 1, 0), jnp.moveaxis(v, 1, 0),
          jnp.moveaxis(decay, 1, 0), jnp.moveaxis(o_intra, 1, 0))
    s_fin, o = lax.scan(step, s0, xs)
    return jnp.moveaxis(o, 0, 1).reshape(B, T, H, DV), s_fin


def memory_kv(mem, mem_norm_g, w_mem_kv, mem_k_norm_g):
    B, M, _ = mem.shape
    kv = rmsnorm(mem, mem_norm_g) @ w_mem_kv
    mk = rmsnorm(kv[..., :H_M * HD_M].reshape(B, M, H_M, HD_M), mem_k_norm_g)
    mv = kv[..., H_M * HD_M:].reshape(B, M, H_M, HD_M)
    return mk, mv


def mem_attend(q, mk, mv):
    B, T = q.shape[:2]
    logits = jnp.einsum('bthd,bmhd->bhtm', q, mk).astype(jnp.float32) * HD_M ** -0.5
    prob = jax.nn.softmax(logits, axis=-1).astype(mv.dtype)
    return jnp.einsum('bhtm,bmhd->bthd', prob, mv).reshape(B, T, H_M * HD_M)


def hmoe(x, ffn_norm_g, w_group_router, b_group_router, w_expert_router, b_expert_router,
         w_exp_gate, w_exp_up, w_exp_down):
    B, T, D = x.shape
    n = B * T
    blk = MOE_BLOCK if n % MOE_BLOCK == 0 else n
    h = rmsnorm(x, ffn_norm_g).reshape(n // blk, blk, D)

    def mix(hb):
        g_logit = (hb @ w_group_router + b_group_router).astype(jnp.float32)
        g_prob = jax.nn.softmax(g_logit, axis=-1)
        g_idx = jnp.argmax(g_logit, axis=-1)
        g_w = jnp.take_along_axis(g_prob, g_idx[:, None], axis=-1)
        e_logit = (hb @ w_expert_router + b_expert_router).astype(jnp.float32)
        e_logit = e_logit.reshape(-1, N_GROUPS, E_PER_GROUP)
        e_logit = jnp.take_along_axis(e_logit, g_idx[:, None, None], axis=1)[:, 0]
        top_p, top_i = lax.top_k(jax.nn.softmax(e_logit, axis=-1), TOP_E)
        top_p = top_p / jnp.sum(top_p, axis=-1, keepdims=True)
        expert_id = g_idx[:, None] * E_PER_GROUP + top_i
        combine = jnp.sum(jax.nn.one_hot(expert_id, N_EXPERTS, dtype=jnp.float32)
                          * (g_w * top_p)[..., None], axis=1)
        hid = jax.nn.silu(jnp.einsum('nd,edf->nef', hb, w_exp_gate)) * jnp.einsum('nd,edf->nef', hb, w_exp_up)
        return jnp.einsum('nef,efd->nd', hid * combine[..., None].astype(hid.dtype), w_exp_down)

    return lax.map(mix, h).reshape(B, T, D)


def layer_forward(x, mem_k, mem_v, past_k, past_v, past_ki, s0, top_k,
                  attn_norm_g, w_in, q_norm_g, k_norm_g, idx_k_norm_g,
                  w_gla_gate_up, b_gla_gate_up, gla_out_norm_g, mem_q_norm_g,
                  w_branch_a, w_branch_g, w_branch_m, w_out, ffn_norm_g,
                  w_group_router, b_group_router, w_expert_router, b_expert_router,
                  w_exp_gate, w_exp_up, w_exp_down):
    B, T, _ = x.shape
    past = 0 if past_k is None else past_k.shape[1]
    h = rmsnorm(x, attn_norm_g)
    (qa, ka, va, qi, ki, wi, qg, kg, vg, g_lr, g_out, qm, g_branch) = split_cols(h @ w_in)

    qa = rmsnorm(qa.reshape(B, T, H_A, HD_A), q_norm_g)
    ka = rmsnorm(ka.reshape(B, T, H_KV, HD_A), k_norm_g)
    va = va.reshape(B, T, H_KV, HD_A)
    qi = qi.reshape(B, T, H_IDX, D_IDX)
    ki = rmsnorm(ki, idx_k_norm_g)
    wi = wi * H_IDX ** -0.5
    if past_k is None:
        k_all, v_all, ki_all = ka, va, ki
    else:
        k_all = jnp.concatenate([past_k, ka], axis=1)
        v_all = jnp.concatenate([past_v, va], axis=1)
        ki_all = jnp.concatenate([past_ki, ki], axis=1)
    q_pos = past + jnp.arange(T)
    k_pos = jnp.arange(past + T)
    if T % Q_BLOCK == 0:
        nb = T // Q_BLOCK
        to_blocks = lambda t: jnp.moveaxis(t.reshape((B, nb, Q_BLOCK) + t.shape[2:]), 1, 0)
        o_a = lax.map(lambda xs: dsa_attend(xs[0], xs[1], xs[2], k_all, v_all, ki_all, xs[3], k_pos, top_k),
                      (to_blocks(qa), to_blocks(qi), to_blocks(wi), q_pos.reshape(nb, Q_BLOCK)))
        o_a = jnp.moveaxis(o_a, 0, 1).reshape(B, T, H_A * HD_A)
    else:
        o_a = dsa_attend(qa, qi, wi, k_all, v_all, ki_all, q_pos, k_pos, top_k)

    qg = qg.reshape(B, T, H_G, DK_G) * DK_G ** -0.5
    kg = kg.reshape(B, T, H_G, DK_G)
    vg = vg.reshape(B, T, H_G, DV_G)
    log_a = jax.nn.log_sigmoid((g_lr @ w_gla_gate_up + b_gla_gate_up).astype(jnp.float32)) / GATE_TEMP
    log_a = log_a.reshape(B, T, H_G, DK_G)
    chunk = CHUNK if T % CHUNK == 0 else T
    o_g, s_new = gla_chunked(qg, kg, vg, log_a, s0, chunk)
    o_g = rmsnorm(o_g, gla_out_norm_g).reshape(B, T, H_G * DV_G) * jax.nn.silu(g_out)

    qm = rmsnorm(qm.reshape(B, T, H_M, HD_M), mem_q_norm_g)
    o_m = mem_attend(qm, mem_k, mem_v)

    gt = jax.nn.sigmoid(g_branch.reshape(B, T, N_BRANCH, D_MODEL))
    merged = (gt[:, :, 0] * (o_a @ w_branch_a) + gt[:, :, 1] * (o_g @ w_branch_g)
              + gt[:, :, 2] * (o_m @ w_branch_m))
    x1 = x + merged @ w_out
    y = x1 + hmoe(x1, ffn_norm_g, w_group_router, b_group_router, w_expert_router, b_expert_router,
                  w_exp_gate, w_exp_up, w_exp_down)
    return y, ka, va, ki, s_new


def setup_inputs(seed: int = 0) -> dict:
    key = jax.random.key(seed)
    ks = iter(jax.random.split(key, 48))
    nrm = lambda shape, scale=1.0: jax.random.normal(next(ks), shape, jnp.float32) * scale
    gain = lambda shape: 1.0 + 0.1 * nrm(shape)
    D = D_MODEL
    return {
        'x_prompt': nrm((BATCH, SEQ, D)),
        'x_sample': nrm((DEC_BATCH, DEC_SEQ, D)),
        'mem_prompt': nrm((BATCH, N_MEM, D)),
        'cache_k': nrm((DEPTH, DEC_BATCH, PAST_LEN, H_KV, HD_A)),
        'cache_v': nrm((DEPTH, DEC_BATCH, PAST_LEN, H_KV, HD_A)),
        'cache_idx_k': nrm((DEPTH, DEC_BATCH, PAST_LEN, D_IDX)),
        'state_gla': nrm((DEPTH, DEC_BATCH, H_G, DK_G, DV_G)),
        'cache_mem_k': nrm((DEPTH, DEC_BATCH, N_MEM, H_M, HD_M)),
        'cache_mem_v': nrm((DEPTH, DEC_BATCH, N_MEM, H_M, HD_M)),
        'attn_norm_g': gain((DEPTH, D)),
        'w_in': nrm((DEPTH, D, D_IN), D ** -0.5),
        'q_norm_g': gain((DEPTH, HD_A)),
        'k_norm_g': gain((DEPTH, HD_A)),
        'idx_k_norm_g': gain((DEPTH, D_IDX)),
        'w_gla_gate_up': nrm((DEPTH, GATE_RANK, H_G * DK_G), GATE_RANK ** -0.5),
        'b_gla_gate_up': nrm((DEPTH, H_G * DK_G), 0.1),
        'gla_out_norm_g': gain((DEPTH, DV_G)),
        'mem_norm_g': gain((DEPTH, D)),
        'w_mem_kv': nrm((DEPTH, D, 2 * H_M * HD_M), D ** -0.5),
        'mem_q_norm_g': gain((DEPTH, HD_M)),
        'mem_k_norm_g': gain((DEPTH, HD_M)),
        'w_branch_a': nrm((DEPTH, H_A * HD_A, D), (H_A * HD_A) ** -0.5),
        'w_branch_g': nrm((DEPTH, H_G * DV_G, D), (H_G * DV_G) ** -0.5),
        'w_branch_m': nrm((DEPTH, H_M * HD_M, D), (H_M * HD_M) ** -0.5),
        'w_out': nrm((DEPTH, D, D), D ** -0.5),
        'ffn_norm_g': gain((DEPTH, D)),
        'w_group_router': nrm((DEPTH, D, N_GROUPS), D ** -0.5),
        'b_group_router': nrm((DEPTH, N_GROUPS), 0.01),
        'w_expert_router': nrm((DEPTH, D, N_EXPERTS), D ** -0.5),
        'b_expert_router': nrm((DEPTH, N_EXPERTS), 0.01),
        'w_exp_gate': nrm((DEPTH, N_EXPERTS, D, D_EXPERT), D ** -0.5),
        'w_exp_up': nrm((DEPTH, N_EXPERTS, D, D_EXPERT), D ** -0.5),
        'w_exp_down': nrm((DEPTH, N_EXPERTS, D_EXPERT, D), D_EXPERT ** -0.5),
    }


def reference(x_prompt, x_sample, mem_prompt, cache_k, cache_v, cache_idx_k, state_gla,
              cache_mem_k, cache_mem_v, attn_norm_g, w_in, q_norm_g, k_norm_g, idx_k_norm_g,
              w_gla_gate_up, b_gla_gate_up, gla_out_norm_g, mem_norm_g, w_mem_kv,
              mem_q_norm_g, mem_k_norm_g, w_branch_a, w_branch_g, w_branch_m, w_out,
              ffn_norm_g, w_group_router, b_group_router, w_expert_router, b_expert_router,
              w_exp_gate, w_exp_up, w_exp_down):
    top_k_prompt = min(TOPK_MAX, x_prompt.shape[1] // 4)
    top_k_sample = min(TOPK_MAX, (cache_k.shape[2] + x_sample.shape[1]) // 4)
    y_p, y_s = x_prompt, x_sample
    kp, vp, kip, sp, mkp, mvp = [], [], [], [], [], []
    ks_, vs_, kis, ss = [], [], [], []
    for l in range(DEPTH):
        lw = (attn_norm_g[l], w_in[l], q_norm_g[l], k_norm_g[l], idx_k_norm_g[l],
              w_gla_gate_up[l], b_gla_gate_up[l], gla_out_norm_g[l], mem_q_norm_g[l],
              w_branch_a[l], w_branch_g[l], w_branch_m[l], w_out[l], ffn_norm_g[l],
              w_group_router[l], b_group_router[l], w_expert_router[l], b_expert_router[l],
              w_exp_gate[l], w_exp_up[l], w_exp_down[l])
        mk_p, mv_p = memory_kv(mem_prompt, mem_norm_g[l], w_mem_kv[l], mem_k_norm_g[l])
        s0 = jnp.zeros((x_prompt.shape[0], H_G, DK_G, DV_G), x_prompt.dtype)
        y_p, k_new, v_new, ki_new, s_new = layer_forward(y_p, mk_p, mv_p, None, None, None, s0,
                                                          top_k_prompt, *lw)
        kp.append(k_new); vp.append(v_new); kip.append(ki_new); sp.append(s_new)
        mkp.append(mk_p); mvp.append(mv_p)
        y_s, k_new, v_new, ki_new, s_new = layer_forward(y_s, cache_mem_k[l], cache_mem_v[l], cache_k[l],
                                                          cache_v[l], cache_idx_k[l], state_gla[l],
                                                          top_k_sample, *lw)
        ks_.append(k_new); vs_.append(v_new); kis.append(ki_new); ss.append(s_new)
    new_k_prompt = jnp.stack(kp)
    new_v_prompt = jnp.stack(vp)
    new_idx_k_prompt = jnp.stack(kip)
    new_state_gla_prompt = jnp.stack(sp)
    new_mem_k_prompt = jnp.stack(mkp)
    new_mem_v_prompt = jnp.stack(mvp)
    new_k_sample = jnp.stack(ks_)
    new_v_sample = jnp.stack(vs_)
    new_idx_k_sample = jnp.stack(kis)
    new_state_gla_sample = jnp.stack(ss)
    return (y_p, y_s, new_k_prompt, new_v_prompt, new_idx_k_prompt, new_state_gla_prompt,
            new_mem_k_prompt, new_mem_v_prompt, new_k_sample, new_v_sample, new_idx_k_sample,
            new_state_gla_sample)
```

```python
import functools

import numpy as np
import jax
import jax.numpy as jnp
from jax import lax
from jax.experimental import pallas as pl
from jax.experimental.pallas import tpu as pltpu

F32 = jnp.float32
BF16 = jnp.bfloat16

D_MODEL = 1024
CHUNK = 64
CHUNK_SHIFT = 6
TOPK_MAX = 256
H_A, HD_A, H_KV = 8, 128, 2
H_IDX, D_IDX = 16, 64
H_G, DK_G, DV_G = 4, 128, 256
GATE_RANK, GATE_TEMP = 16, 16.0
H_M, HD_M = 4, 256
N_GROUPS, E_PER_GROUP = 4, 8
N_EXPERTS = N_GROUPS * E_PER_GROUP
D_EXPERT = 256
EPS = 1e-6
NEG_INF = -1e30
LANES = 128

QA_OFF, QI_OFF, QG_OFF, KG_OFF, VG_OFF = 0, 1024, 2048, 2560, 3072
GO_OFF, QM_OFF, GB_OFF, KA_OFF, VA_OFF, TAIL_OFF = 4096, 5120, 6144, 9216, 9472, 9728
TAIL_KI, TAIL_WI, TAIL_GLR = 0, 64, 80
C_PAD = 10240
_SRC = dict(qa=(0, 1024), ka=(1024, 1280), va=(1280, 1536), qi=(1536, 2560), ki=(2560, 2624),
            wi=(2624, 2640), qg=(2640, 3152), kg=(3152, 3664), vg=(3664, 4688), glr=(4688, 4704),
            go=(4704, 5728), qm=(5728, 6752), gb=(6752, 9824))
_DST_ORDER = ("qa", "qi", "qg", "kg", "vg", "go", "qm", "gb", "ka", "va", "ki", "wi", "glr")

INT_MIN = -2 ** 31
_KEY_VALID = int(np.float32(NEG_INF * 0.5).view(np.int32)) ^ 0x7FFFFFFF
if _KEY_VALID >= 2 ** 31:
    _KEY_VALID -= 2 ** 32

VMEM_LIMIT = 56 * 1024 * 1024


def _params(*sem):
    return pltpu.CompilerParams(dimension_semantics=sem, vmem_limit_bytes=VMEM_LIMIT)


def _rms(x, g):
    return x * lax.rsqrt(jnp.mean(x * x, axis=-1, keepdims=True) + EPS) * g


def _dot(a, b):
    return jnp.dot(a, b, preferred_element_type=F32)


def _dot_nt(a, b):
    return lax.dot_general(a, b, (((1,), (1,)), ((), ())), preferred_element_type=F32)


def _dot_tn(a, b):
    return lax.dot_general(a, b, (((0,), (0,)), ((), ())), preferred_element_type=F32)


def _norm_proj_kernel(x_ref, g_ref, w_ref, o_ref, h_ref):
    @pl.when(pl.program_id(1) == 0)
    def _():
        h_ref[...] = _rms(x_ref[...], g_ref[...]).astype(BF16)

    o_ref[...] = _dot(h_ref[...], w_ref[...])


def _norm_proj(x, g, w, tm, tn):
    n, d = x.shape
    c = w.shape[1]
    return pl.pallas_call(
        _norm_proj_kernel,
        grid=(n // tm, c // tn),
        in_specs=[pl.BlockSpec((tm, d), lambda i, j: (i, 0)),
                  pl.BlockSpec((1, d), lambda i, j: (0, 0)),
                  pl.BlockSpec((d, tn), lambda i, j: (0, j))],
        out_specs=pl.BlockSpec((tm, tn), lambda i, j: (i, j)),
        out_shape=jax.ShapeDtypeStruct((n, c), F32),
        scratch_shapes=[pltpu.VMEM((tm, d), BF16)],
        compiler_params=_params("parallel", "arbitrary"),
        name="norm_proj",
    )(x, g, w)


def _kv_post_kernel(kv_ref, tail_ref, kg_ref, ig_ref, k_ref, v_ref, ki_ref):
    kv = kv_ref[...]
    for h in range(H_KV):
        k_ref[:, h * HD_A:(h + 1) * HD_A] = _rms(kv[:, h * HD_A:(h + 1) * HD_A], kg_ref[...])
    v_ref[...] = kv[:, H_KV * HD_A:]
    ki_ref[...] = _rms(tail_ref[:, TAIL_KI:TAIL_KI + D_IDX], ig_ref[...])


def _kv_post(p, k_g, ik_g, tm):
    n = p.shape[0]
    w = 2 * H_KV * HD_A
    return pl.pallas_call(
        _kv_post_kernel,
        grid=(n // tm,),
        in_specs=[pl.BlockSpec((tm, w), lambda i: (i, KA_OFF // w)),
                  pl.BlockSpec((tm, LANES), lambda i: (i, TAIL_OFF // LANES)),
                  pl.BlockSpec((1, HD_A), lambda i: (0, 0)),
                  pl.BlockSpec((1, D_IDX), lambda i: (0, 0))],
        out_specs=[pl.BlockSpec((tm, H_KV * HD_A), lambda i: (i, 0)),
                   pl.BlockSpec((tm, H_KV * HD_A), lambda i: (i, 0)),
                   pl.BlockSpec((tm, D_IDX), lambda i: (i, 0))],
        out_shape=[jax.ShapeDtypeStruct((n, H_KV * HD_A), F32),
                   jax.ShapeDtypeStruct((n, H_KV * HD_A), F32),
                   jax.ShapeDtypeStruct((n, D_IDX), F32)],
        compiler_params=_params("parallel"),
        name="kv_post",
    )(p, p, k_g, ik_g)


def _to_key(x):
    x = jnp.where(x == 0.0, 0.0, x)
    b = lax.bitcast_convert_type(x, jnp.int32)
    return jnp.where(b < 0, b ^ 0x7FFFFFFF, b)


def _dsa_kernel(qa_ref, qi_ref, tail_ref, k_ref, v_ref, ki_ref, qg_ref, o_ref,
                key_ref, m_ref, l_ref, acc_ref, *, tq, tk, s_valid, past, top_k):
    i = pl.program_id(1)
    q0 = past + i * tq
    lim = jnp.minimum((((q0 + tq - 1) >> CHUNK_SHIFT) + 1) << CHUNK_SHIFT, s_valid)
    n_tiles = (lim + tk - 1) // tk
    lane = lax.broadcasted_iota(jnp.int32, (tq, LANES), 1)
    q_chunk = (q0 + lax.broadcasted_iota(jnp.int32, (tq, 1), 0)) >> CHUNK_SHIFT

    qi = qi_ref[...]
    qi_h = []
    for j in range(H_IDX // 2):
        pair = qi[:, j * LANES:(j + 1) * LANES] * (D_IDX ** -0.5)
        qi_h.append(jnp.where(lane < D_IDX, pair, 0.0).astype(BF16))
        qi_h.append(jnp.where(lane >= D_IDX, pair, 0.0).astype(BF16))
    wi = tail_ref[:, TAIL_WI:TAIL_WI + H_IDX] * (H_IDX ** -0.5)

    def tile_off(j):
        return pl.multiple_of(j * tk, tk)

    def score_tile(j, carry):
        off = tile_off(j)
        kt = ki_ref[pl.ds(off, tk), :]
        acc = jnp.zeros((tq, tk), F32)
        for h in range(H_IDX):
            acc = acc + jnp.maximum(_dot_nt(qi_h[h], kt), 0.0) * wi[:, h:h + 1]
        kpos = off + lax.broadcasted_iota(jnp.int32, (1, tk), 1)
        adm = ((kpos >> CHUNK_SHIFT) <= q_chunk) & (kpos < s_valid)
        key_ref[:, pl.ds(off, tk)] = _to_key(jnp.where(adm, acc, NEG_INF))
        return carry

    lax.fori_loop(0, n_tiles, score_tile, 0)

    def count(pred):
        def body(j, c):
            m = jnp.where(pred(key_ref[:, pl.ds(tile_off(j), tk)]), 1.0, 0.0)
            part = m[:, 0:LANES]
            for u in range(1, tk // LANES):
                part = part + m[:, u * LANES:(u + 1) * LANES]
            return c + part
        c = lax.fori_loop(0, n_tiles, body, jnp.zeros((tq, LANES), F32))
        return jnp.sum(c, axis=-1, keepdims=True)

    def bit_step(it, v):
        trial = v ^ lax.shift_left(jnp.int32(1), 31 - it)
        return jnp.where(count(lambda kt: kt >= trial) >= float(top_k), trial, v)

    v = lax.fori_loop(0, 32, bit_step, jnp.full((tq, 1), INT_MIN, jnp.int32))

    excess = (count(lambda kt: kt >= v) > float(top_k)) & (v > _KEY_VALID)

    @pl.when(jnp.max(jnp.where(excess, 1.0, 0.0)) > 0.0)
    def _():
        need = float(top_k) - count(lambda kt: kt > v)
        r_i = lax.broadcasted_iota(jnp.int32, (tk, tk), 0)
        c_i = lax.broadcasted_iota(jnp.int32, (tk, tk), 1)
        before = jnp.where(r_i < c_i, 1.0, 0.0).astype(BF16)

        def body(j, seen):
            off = tile_off(j)
            kt = key_ref[:, pl.ds(off, tk)]
            eq = kt == v
            eqf = jnp.where(eq, 1.0, 0.0)
            rank = seen + _dot(eqf.astype(BF16), before)
            key_ref[:, pl.ds(off, tk)] = jnp.where(eq & (rank >= need), v - 1, kt)
            return seen + jnp.sum(eqf, axis=-1, keepdims=True)

        lax.fori_loop(0, n_tiles, body, jnp.zeros((tq, 1), F32))

    qa = qa_ref[...]
    qs = [(_rms(qa[:, h * HD_A:(h + 1) * HD_A], qg_ref[...]) * (HD_A ** -0.5)).astype(BF16)
          for h in range(H_A)]
    m_ref[...] = jnp.full(m_ref.shape, NEG_INF, F32)
    l_ref[...] = jnp.zeros(l_ref.shape, F32)
    acc_ref[...] = jnp.zeros(acc_ref.shape, F32)
    rep = H_A // H_KV

    def att_tile(j, carry):
        off = tile_off(j)
        kt_key = key_ref[:, pl.ds(off, tk)]
        mask = (kt_key >= v) & (kt_key > _KEY_VALID)
        for g in range(H_KV):
            kt = k_ref[pl.ds(off, tk), g * HD_A:(g + 1) * HD_A]
            vt = v_ref[pl.ds(off, tk), g * HD_A:(g + 1) * HD_A]
            for r in range(rep):
                h = g * rep + r
                s = jnp.where(mask, _dot_nt(qs[h], kt), NEG_INF)
                m_old = m_ref[h]
                m_new = jnp.maximum(m_old, jnp.max(s, axis=-1, keepdims=True))
                p = jnp.where(mask, jnp.exp(s - m_new), 0.0)
                alpha = jnp.exp(m_old - m_new)
                l_ref[h] = alpha * l_ref[h] + jnp.sum(p, axis=-1, keepdims=True)
                acc_ref[h] = alpha * acc_ref[h] + _dot(p.astype(BF16), vt)
                m_ref[h] = m_new
        return carry

    lax.fori_loop(0, n_tiles, att_tile, 0)
    for h in range(H_A):
        o_ref[:, h * HD_A:(h + 1) * HD_A] = (acc_ref[h] / l_ref[h]).astype(o_ref.dtype)


def _dsa(p, k_all, v_all, ki2, q_g, *, b, t, tq, tk, s_valid, past, top_k):
    nq = t // tq
    s_pad = k_all.shape[1]
    wq = H_A * HD_A
    kern = functools.partial(_dsa_kernel, tq=tq, tk=tk, s_valid=s_valid, past=past, top_k=top_k)
    return pl.pallas_call(
        kern,
        grid=(b, nq),
        in_specs=[pl.BlockSpec((tq, wq), lambda bb, i: (bb * nq + i, QA_OFF // wq)),
                  pl.BlockSpec((tq, wq), lambda bb, i: (bb * nq + i, QI_OFF // wq)),
                  pl.BlockSpec((tq, LANES), lambda bb, i: (bb * nq + i, TAIL_OFF // LANES)),
                  pl.BlockSpec((None, s_pad, H_KV * HD_A), lambda bb, i: (bb, 0, 0)),
                  pl.BlockSpec((None, s_pad, H_KV * HD_A), lambda bb, i: (bb, 0, 0)),
                  pl.BlockSpec((None, s_pad, LANES), lambda bb, i: (bb, 0, 0)),
                  pl.BlockSpec((1, HD_A), lambda bb, i: (0, 0))],
        out_specs=pl.BlockSpec((tq, wq), lambda bb, i: (bb * nq + i, 0)),
        out_shape=jax.ShapeDtypeStruct((b * t, wq), BF16),
        scratch_shapes=[pltpu.VMEM((tq, s_pad), jnp.int32),
                        pltpu.VMEM((H_A, tq, 1), F32),
                        pltpu.VMEM((H_A, tq, 1), F32),
                        pltpu.VMEM((H_A, tq, HD_A), F32)],
        compiler_params=_params("parallel", "arbitrary"),
        name="dsa",
    )(p, p, p, k_all, v_all, ki2, q_g)


def _gla_kernel(q_ref, k_ref, v_ref, go_ref, tail_ref, wup_ref, bup_ref, ng_ref, s0_ref,
                o_ref, s_ref, la_ref, *, tt, chunk):
    @pl.when(pl.program_id(2) == 0)
    def _():
        s_ref[...] = s0_ref[...]

    glr = tail_ref[:, TAIL_GLR:TAIL_GLR + GATE_RANK].astype(BF16)
    z = _dot(glr, wup_ref[...]) + bup_ref[...]
    la_ref[...] = (jnp.minimum(z, 0.0) - jnp.log1p(jnp.exp(-jnp.abs(z)))) / GATE_TEMP

    row = lax.broadcasted_iota(jnp.int32, (chunk, DK_G), 0)
    causal = (lax.broadcasted_iota(jnp.int32, (chunk, chunk), 0)
              >= lax.broadcasted_iota(jnp.int32, (chunk, chunk), 1))
    eye = (lax.broadcasted_iota(jnp.int32, (DK_G, DK_G), 0)
           == lax.broadcasted_iota(jnp.int32, (DK_G, DK_G), 1))

    def body(c, carry):
        off = pl.multiple_of(c * chunk, chunk)
        rows = pl.ds(off, chunk)
        bcum = la_ref[rows, :]
        step = 1
        while step < chunk:
            bcum = bcum + jnp.where(row >= step, pltpu.roll(bcum, step, axis=0), 0.0)
            step *= 2
        b_last = bcum[chunk - 1:chunk, :]
        q = q_ref[rows, :] * (DK_G ** -0.5)
        k = k_ref[rows, :]
        vv = v_ref[rows, :].astype(BF16)
        qe = (q * jnp.exp(bcum)).astype(BF16)
        ke = (k * jnp.exp(-bcum)).astype(BF16)
        kd = (k * jnp.exp(b_last - bcum)).astype(BF16)
        att = jnp.where(causal, _dot_nt(qe, ke), 0.0).astype(BF16)
        state = s_ref[...]
        o = _dot(att, vv) + _dot(qe, state.astype(BF16))
        decay = jnp.exp(b_last)
        decay_col = jnp.sum(jnp.where(eye, jnp.broadcast_to(decay, (DK_G, DK_G)), 0.0),
                            axis=1, keepdims=True)
        s_ref[...] = state * decay_col + _dot_tn(kd, vv)
        go = go_ref[rows, :]
        o_ref[rows, :] = (_rms(o, ng_ref[...]) * (go * jax.nn.sigmoid(go))).astype(o_ref.dtype)
        return carry

    lax.fori_loop(0, tt // chunk, body, 0)


def _gla(p, w_up, b_up, n_g, s0, *, b, t, tt, chunk):
    nt = t // tt
    kern = functools.partial(_gla_kernel, tt=tt, chunk=chunk)
    row = lambda bb, h, i: bb * nt + i
    return pl.pallas_call(
        kern,
        grid=(b, H_G, nt),
        in_specs=[pl.BlockSpec((tt, DK_G), lambda bb, h, i: (row(bb, h, i), QG_OFF // DK_G + h)),
                  pl.BlockSpec((tt, DK_G), lambda bb, h, i: (row(bb, h, i), KG_OFF // DK_G + h)),
                  pl.BlockSpec((tt, DV_G), lambda bb, h, i: (row(bb, h, i), VG_OFF // DV_G + h)),
                  pl.BlockSpec((tt, DV_G), lambda bb, h, i: (row(bb, h, i), GO_OFF // DV_G + h)),
                  pl.BlockSpec((tt, LANES), lambda bb, h, i: (row(bb, h, i), TAIL_OFF // LANES)),
                  pl.BlockSpec((None, GATE_RANK, DK_G), lambda bb, h, i: (h, 0, 0)),
                  pl.BlockSpec((None, 1, DK_G), lambda bb, h, i: (h, 0, 0)),
                  pl.BlockSpec((1, DV_G), lambda bb, h, i: (0, 0)),
                  pl.BlockSpec((None, None, DK_G, DV_G), lambda bb, h, i: (bb, h, 0, 0))],
        out_specs=[pl.BlockSpec((tt, DV_G), lambda bb, h, i: (row(bb, h, i), h)),
                   pl.BlockSpec((None, None, DK_G, DV_G), lambda bb, h, i: (bb, h, 0, 0))],
        out_shape=[jax.ShapeDtypeStruct((b * t, H_G * DV_G), BF16),
                   jax.ShapeDtypeStruct((b, H_G, DK_G, DV_G), F32)],
        scratch_shapes=[pltpu.VMEM((tt, DK_G), F32)],
        compiler_params=_params("parallel", "parallel", "arbitrary"),
        name="gla",
    )(p, p, p, p, p, w_up, b_up, n_g, s0)


def _mem_kv_kernel(x_ref, g_ref, w_ref, kg_ref, mk_ref, mv_ref):
    kv = _dot(_rms(x_ref[...], g_ref[...]).astype(BF16), w_ref[...])
    for h in range(H_M):
        mk_ref[:, h * HD_M:(h + 1) * HD_M] = _rms(kv[:, h * HD_M:(h + 1) * HD_M], kg_ref[...])
    mv_ref[...] = kv[:, H_M * HD_M:]


def _mem_kv(mem, g, w, k_g, tm):
    n, d = mem.shape
    wk = H_M * HD_M
    return pl.pallas_call(
        _mem_kv_kernel,
        grid=(n // tm,),
        in_specs=[pl.BlockSpec((tm, d), lambda i: (i, 0)),
                  pl.BlockSpec((1, d), lambda i: (0, 0)),
                  pl.BlockSpec((d, 2 * wk), lambda i: (0, 0)),
                  pl.BlockSpec((1, HD_M), lambda i: (0, 0))],
        out_specs=[pl.BlockSpec((tm, wk), lambda i: (i, 0)),
                   pl.BlockSpec((tm, wk), lambda i: (i, 0))],
        out_shape=[jax.ShapeDtypeStruct((n, wk), F32), jax.ShapeDtypeStruct((n, wk), F32)],
        compiler_params=_params("parallel"),
        name="mem_kv",
    )(mem, g, w, k_g)


def _mem_attn_kernel(q_ref, mk_ref, mv_ref, g_ref, o_ref):
    qm = q_ref[...]
    for h in range(H_M):
        cols = slice(h * HD_M, (h + 1) * HD_M)
        q = (_rms(qm[:, cols], g_ref[...]) * (HD_M ** -0.5)).astype(BF16)
        s = _dot_nt(q, mk_ref[:, cols])
        e = jnp.exp(s - jnp.max(s, axis=-1, keepdims=True))
        prob = e / jnp.sum(e, axis=-1, keepdims=True)
        o_ref[:, cols] = _dot(prob.astype(BF16), mv_ref[:, cols]).astype(o_ref.dtype)


def _mem_attn(p, mk, mv, q_g, *, b, t, tt):
    nt = t // tt
    wq = H_M * HD_M
    n_mem = mk.shape[1]
    return pl.pallas_call(
        _mem_attn_kernel,
        grid=(b, nt),
        in_specs=[pl.BlockSpec((tt, wq), lambda bb, i: (bb * nt + i, QM_OFF // wq)),
                  pl.BlockSpec((None, n_mem, wq), lambda bb, i: (bb, 0, 0)),
                  pl.BlockSpec((None, n_mem, wq), lambda bb, i: (bb, 0, 0)),
                  pl.BlockSpec((1, HD_M), lambda bb, i: (0, 0))],
        out_specs=pl.BlockSpec((tt, wq), lambda bb, i: (bb * nt + i, 0)),
        out_shape=jax.ShapeDtypeStruct((b * t, wq), BF16),
        compiler_params=_params("parallel", "arbitrary"),
        name="mem_attn",
    )(p, mk, mv, q_g)


def _merge_kernel(x_ref, oa_ref, og_ref, om_ref, gb_ref, wa_ref, wg_ref, wm_ref, wo_ref,
                  fg_ref, wr_ref, br_ref, x1_ref, h2_ref, cb_ref):
    d = D_MODEL
    merged = (jax.nn.sigmoid(gb_ref[:, 0:d]) * _dot(oa_ref[...], wa_ref[...])
              + jax.nn.sigmoid(gb_ref[:, d:2 * d]) * _dot(og_ref[...], wg_ref[...])
              + jax.nn.sigmoid(gb_ref[:, 2 * d:3 * d]) * _dot(om_ref[...], wm_ref[...]))
    x1 = x_ref[...] + _dot(merged.astype(BF16), wo_ref[...])
    x1_ref[...] = x1
    hb = _rms(x1, fg_ref[...]).astype(BF16)
    h2_ref[...] = hb

    lg = _dot(hb, wr_ref[...]) + br_ref[...]
    lane = lax.broadcasted_iota(jnp.int32, lg.shape, 1).astype(F32)
    big = 1e9
    gmask = (lane >= N_EXPERTS) & (lane < N_EXPERTS + N_GROUPS)
    gl = jnp.where(gmask, lg, NEG_INF)
    gmax = jnp.max(gl, axis=-1, keepdims=True)
    gidx = jnp.min(jnp.where(gmask & (gl == gmax), lane, big), axis=-1, keepdims=True) - N_EXPERTS
    g_w = 1.0 / jnp.sum(jnp.where(gmask, jnp.exp(gl - gmax), 0.0), axis=-1, keepdims=True)
    lo = gidx * E_PER_GROUP
    emask = (lane >= lo) & (lane < lo + E_PER_GROUP)
    el = jnp.where(emask, lg, NEG_INF)
    ee = jnp.where(emask, jnp.exp(el - jnp.max(el, axis=-1, keepdims=True)), 0.0)
    ep = ee / jnp.sum(ee, axis=-1, keepdims=True)
    p1 = jnp.max(jnp.where(emask, ep, -1.0), axis=-1, keepdims=True)
    i1 = jnp.min(jnp.where(emask & (ep == p1), lane, big), axis=-1, keepdims=True)
    rest = emask & (lane != i1)
    p2 = jnp.max(jnp.where(rest, ep, -1.0), axis=-1, keepdims=True)
    i2 = jnp.min(jnp.where(rest & (ep == p2), lane, big), axis=-1, keepdims=True)
    den = p1 + p2
    cb_ref[...] = (jnp.where(lane == i1, g_w * (p1 / den), 0.0)
                   + jnp.where(lane == i2, g_w * (p2 / den), 0.0))


def _merge(x, oa, og, om, p, wa, wg, wm, wo, f_g, w_r, b_r, tm):
    n, d = x.shape
    full = lambda shape: pl.BlockSpec(shape, lambda i: (0, 0))
    rowblk = lambda w: pl.BlockSpec((tm, w), lambda i: (i, 0))
    return pl.pallas_call(
        _merge_kernel,
        grid=(n // tm,),
        in_specs=[rowblk(d), rowblk(d), rowblk(d), rowblk(d),
                  pl.BlockSpec((tm, 3 * d), lambda i: (i, GB_OFF // (3 * d))),
                  full((d, d)), full((d, d)), full((d, d)), full((d, d)),
                  full((1, d)), full((d, LANES)), full((1, LANES))],
        out_specs=[rowblk(d), rowblk(d), rowblk(LANES)],
        out_shape=[jax.ShapeDtypeStruct((n, d), F32),
                   jax.ShapeDtypeStruct((n, d), BF16),
                   jax.ShapeDtypeStruct((n, LANES), F32)],
        compiler_params=_params("parallel"),
        name="merge",
    )(x, oa, og, om, p, wa, wg, wm, wo, f_g, w_r, b_r)


def _moe_kernel(h_ref, cb_ref, x1_ref, wg_ref, wu_ref, wd_ref, y_ref):
    e = pl.program_id(1)

    @pl.when(e == 0)
    def _():
        y_ref[...] = x1_ref[...]

    h = h_ref[...]
    gate = _dot(h, wg_ref[...])
    up = _dot(h, wu_ref[...])
    cb = cb_ref[...]
    lane = lax.broadcasted_iota(jnp.int32, cb.shape, 1)
    c = jnp.sum(jnp.where(lane == e, cb, 0.0), axis=-1, keepdims=True)
    hid = (gate * jax.nn.sigmoid(gate)) * up * c
    y_ref[...] += _dot(hid.astype(BF16), wd_ref[...])


def _moe(h2, cb, x1, wg, wu, wd, tm):
    n, d = x1.shape
    return pl.pallas_call(
        _moe_kernel,
        grid=(n // tm, N_EXPERTS),
        in_specs=[pl.BlockSpec((tm, d), lambda i, e: (i, 0)),
                  pl.BlockSpec((tm, LANES), lambda i, e: (i, 0)),
                  pl.BlockSpec((tm, d), lambda i, e: (i, 0)),
                  pl.BlockSpec((None, d, D_EXPERT), lambda i, e: (e, 0, 0)),
                  pl.BlockSpec((None, d, D_EXPERT), lambda i, e: (e, 0, 0)),
                  pl.BlockSpec((None, D_EXPERT, d), lambda i, e: (e, 0, 0))],
        out_specs=pl.BlockSpec((tm, d), lambda i, e: (i, 0)),
        out_shape=jax.ShapeDtypeStruct((n, d), F32),
        compiler_params=_params("parallel", "arbitrary"),
        name="moe",
    )(h2, cb, x1, wg, wu, wd)


def _round_up(n, m):
    return (n + m - 1) // m * m


def _layer(x, mk, mv, past_k, past_v, past_ki, s0, w):
    b, t, d = x.shape
    n = b * t
    x2 = x.reshape(n, d)
    p = _norm_proj(x2, w["attn_g"], w["w_cat"], min(1024, n), 1024)
    ka, va, ki = _kv_post(p, w["k_g"], w["ik_g"], min(512, n))

    k3 = ka.reshape(b, t, H_KV * HD_A)
    v3 = va.reshape(b, t, H_KV * HD_A)
    ki3 = ki.reshape(b, t, D_IDX)
    past = 0
    if past_k is not None:
        past = past_k.shape[1]
        k3 = jnp.concatenate([past_k.reshape(b, past, -1), k3], axis=1)
        v3 = jnp.concatenate([past_v.reshape(b, past, -1), v3], axis=1)
        ki3 = jnp.concatenate([past_ki, ki3], axis=1)
    s_valid = past + t
    tk = 512
    pad = ((0, 0), (0, _round_up(s_valid, tk) - s_valid), (0, 0))
    k_all = jnp.pad(k3.astype(BF16), pad)
    v_all = jnp.pad(v3.astype(BF16), pad)
    ki2 = jnp.pad(jnp.concatenate([ki3, ki3], axis=-1).astype(BF16), pad)
    oa = _dsa(p, k_all, v_all, ki2, w["q_g"], b=b, t=t, tq=min(128, t), tk=tk,
              s_valid=s_valid, past=past, top_k=min(TOPK_MAX, s_valid // 4))

    chunk = CHUNK if t % CHUNK == 0 else t
    tt = min(512, t)
    og, s_new = _gla(p, w["w_up"], w["b_up"], w["gla_g"], s0, b=b, t=t, tt=tt, chunk=chunk)
    om = _mem_attn(p, mk, mv, w["mq_g"], b=b, t=t, tt=tt)
    x1, h2, cb = _merge(x2, oa, og, om, p, w["w_a"], w["w_g"], w["w_m"], w["w_o"],
                        w["ffn_g"], w["w_r"], w["b_r"], min(256, n))
    y = _moe(h2, cb, x1, w["w_eg"], w["w_eu"], w["w_ed"], min(1024, n))
    return (y.reshape(b, t, d), ka.reshape(b, t, H_KV, HD_A), va.reshape(b, t, H_KV, HD_A),
            ki.reshape(b, t, D_IDX), s_new)


def _prep_weights(l, attn_norm_g, w_in, q_norm_g, k_norm_g, idx_k_norm_g, w_gla_gate_up,
                  b_gla_gate_up, gla_out_norm_g, mem_q_norm_g, w_branch_a, w_branch_g,
                  w_branch_m, w_out, ffn_norm_g, w_group_router, b_group_router,
                  w_expert_router, b_expert_router, w_exp_gate, w_exp_up, w_exp_down):
    d = D_MODEL
    wi = w_in[l]
    cols = [wi[:, _SRC[name][0]:_SRC[name][1]] for name in _DST_ORDER]
    used = sum(c.shape[1] for c in cols)
    w_cat = jnp.concatenate(cols + [jnp.zeros((d, C_PAD - used), F32)], axis=1).astype(BF16)
    pad_r = LANES - N_EXPERTS - N_GROUPS
    w_r = jnp.concatenate([w_expert_router[l], w_group_router[l], jnp.zeros((d, pad_r), F32)], axis=1)
    b_r = jnp.concatenate([b_expert_router[l], b_group_router[l], jnp.zeros((pad_r,), F32)])
    return dict(
        attn_g=attn_norm_g[l][None], w_cat=w_cat, q_g=q_norm_g[l][None], k_g=k_norm_g[l][None],
        ik_g=idx_k_norm_g[l][None],
        w_up=w_gla_gate_up[l].reshape(GATE_RANK, H_G, DK_G).transpose(1, 0, 2).astype(BF16),
        b_up=b_gla_gate_up[l].reshape(H_G, 1, DK_G), gla_g=gla_out_norm_g[l][None],
        mq_g=mem_q_norm_g[l][None],
        w_a=w_branch_a[l].astype(BF16), w_g=w_branch_g[l].astype(BF16),
        w_m=w_branch_m[l].astype(BF16), w_o=w_out[l].astype(BF16),
        ffn_g=ffn_norm_g[l][None], w_r=w_r.astype(BF16), b_r=b_r[None],
        w_eg=w_exp_gate[l].astype(BF16), w_eu=w_exp_up[l].astype(BF16),
        w_ed=w_exp_down[l].astype(BF16))


def kernel(x_prompt, x_sample, mem_prompt, cache_k, cache_v, cache_idx_k, state_gla, cache_mem_k, cache_mem_v, attn_norm_g, w_in, q_norm_g, k_norm_g, idx_k_norm_g, w_gla_gate_up, b_gla_gate_up, gla_out_norm_g, mem_norm_g, w_mem_kv, mem_q_norm_g, mem_k_norm_g, w_branch_a, w_branch_g, w_branch_m, w_out, ffn_norm_g, w_group_router, b_group_router, w_expert_router, b_expert_router, w_exp_gate, w_exp_up, w_exp_down):
    depth = w_in.shape[0]
    y_p, y_s = x_prompt, x_sample
    outs = [[] for _ in range(10)]
    for l in range(depth):
        w = _prep_weights(l, attn_norm_g, w_in, q_norm_g, k_norm_g, idx_k_norm_g, w_gla_gate_up,
                          b_gla_gate_up, gla_out_norm_g, mem_q_norm_g, w_branch_a, w_branch_g,
                          w_branch_m, w_out, ffn_norm_g, w_group_router, b_group_router,
                          w_expert_router, b_expert_router, w_exp_gate, w_exp_up, w_exp_down)
        bp, n_mem, d = mem_prompt.shape
        mk_p, mv_p = _mem_kv(mem_prompt.reshape(bp * n_mem, d), mem_norm_g[l][None],
                             w_mem_kv[l].astype(BF16), mem_k_norm_g[l][None], min(256, bp * n_mem))
        mk_p = mk_p.reshape(bp, n_mem, H_M, HD_M)
        mv_p = mv_p.reshape(bp, n_mem, H_M, HD_M)
        s0 = jnp.zeros((bp, H_G, DK_G, DV_G), F32)
        y_p, k_new, v_new, ki_new, s_new = _layer(
            y_p, mk_p.reshape(bp, n_mem, -1).astype(BF16), mv_p.reshape(bp, n_mem, -1).astype(BF16),
            None, None, None, s0, w)
        for lst, val in zip(outs[:6], (k_new, v_new, ki_new, s_new, mk_p, mv_p)):
            lst.append(val)
        bs = x_sample.shape[0]
        y_s, k_new, v_new, ki_new, s_new = _layer(
            y_s, cache_mem_k[l].reshape(bs, n_mem, -1).astype(BF16),
            cache_mem_v[l].reshape(bs, n_mem, -1).astype(BF16),
            cache_k[l], cache_v[l], cache_idx_k[l], state_gla[l], w)
        for lst, val in zip(outs[6:], (k_new, v_new, ki_new, s_new)):
            lst.append(val)
    return (y_p, y_s) + tuple(jnp.stack(o) for o in outs)
```

```python
import functools

import numpy as np
import jax
import jax.numpy as jnp
from jax import lax
from jax.experimental import pallas as pl
from jax.experimental.pallas import tpu as pltpu

F32 = jnp.float32
BF16 = jnp.bfloat16

D_MODEL = 1024
CHUNK = 64
CHUNK_SHIFT = 6
TOPK_MAX = 256
H_A, HD_A, H_KV = 8, 128, 2
H_IDX, D_IDX = 16, 64
H_G, DK_G, DV_G = 4, 128, 256
GATE_RANK, GATE_TEMP = 16, 16.0
H_M, HD_M = 4, 256
N_GROUPS, E_PER_GROUP = 4, 8
N_EXPERTS = N_GROUPS * E_PER_GROUP
D_EXPERT = 256
EPS = 1e-6
NEG_INF = -1e30
LANES = 128
LOG2E = 1.4426950408889634

QA_OFF, QI_OFF, QG_OFF, KG_OFF, VG_OFF = 0, 1024, 2048, 2560, 3072
GO_OFF, QM_OFF, GB_OFF, KA_OFF, VA_OFF, TAIL_OFF = 4096, 5120, 6144, 9216, 9472, 9728
TAIL_KI, TAIL_WI, TAIL_GLR = 0, 64, 80
C_PAD = 10240
_SRC = dict(qa=(0, 1024), ka=(1024, 1280), va=(1280, 1536), qi=(1536, 2560), ki=(2560, 2624),
            wi=(2624, 2640), qg=(2640, 3152), kg=(3152, 3664), vg=(3664, 4688), glr=(4688, 4704),
            go=(4704, 5728), qm=(5728, 6752), gb=(6752, 9824))
_DST_ORDER = ("qa", "qi", "qg", "kg", "vg", "go", "qm", "gb", "ka", "va", "ki", "wi", "glr")

INT_MIN = -2 ** 31
_KEY_VALID = int(np.float32(NEG_INF * 0.5).view(np.int32)) ^ 0x7FFFFFFF
if _KEY_VALID >= 2 ** 31:
    _KEY_VALID -= 2 ** 32

VMEM_LIMIT = 56 * 1024 * 1024


def _params(*sem):
    return pltpu.CompilerParams(dimension_semantics=sem, vmem_limit_bytes=VMEM_LIMIT)


def _rms(x, g):
    return x * lax.rsqrt(jnp.mean(x * x, axis=-1, keepdims=True) + EPS) * g


def _dot(a, b):
    return jnp.dot(a, b, preferred_element_type=F32)


def _dot_nt(a, b):
    return lax.dot_general(a, b, (((1,), (1,)), ((), ())), preferred_element_type=F32)


def _dot_tn(a, b):
    return lax.dot_general(a, b, (((0,), (0,)), ((), ())), preferred_element_type=F32)


def _norm_proj_kernel(x_ref, g_ref, w_ref, o_ref, h_ref):
    @pl.when(pl.program_id(1) == 0)
    def _():
        h_ref[...] = _rms(x_ref[...], g_ref[...]).astype(BF16)

    o_ref[...] = _dot(h_ref[...], w_ref[...])


def _norm_proj(x, g, w, tm, tn):
    n, d = x.shape
    c = w.shape[1]
    return pl.pallas_call(
        _norm_proj_kernel,
        grid=(n // tm, c // tn),
        in_specs=[pl.BlockSpec((tm, d), lambda i, j: (i, 0)),
                  pl.BlockSpec((1, d), lambda i, j: (0, 0)),
                  pl.BlockSpec((d, tn), lambda i, j: (0, j))],
        out_specs=pl.BlockSpec((tm, tn), lambda i, j: (i, j)),
        out_shape=jax.ShapeDtypeStruct((n, c), F32),
        scratch_shapes=[pltpu.VMEM((tm, d), BF16)],
        compiler_params=_params("parallel", "arbitrary"),
        name="norm_proj",
    )(x, g, w)


def _kv_post_kernel(kv_ref, tail_ref, kg_ref, ig_ref, k_ref, v_ref, ki_ref):
    kv = kv_ref[...]
    for h in range(H_KV):
        k_ref[:, h * HD_A:(h + 1) * HD_A] = _rms(kv[:, h * HD_A:(h + 1) * HD_A], kg_ref[...])
    v_ref[...] = kv[:, H_KV * HD_A:]
    ki_ref[...] = _rms(tail_ref[:, TAIL_KI:TAIL_KI + D_IDX], ig_ref[...])


def _kv_post(p, k_g, ik_g, tm):
    n = p.shape[0]
    w = 2 * H_KV * HD_A
    return pl.pallas_call(
        _kv_post_kernel,
        grid=(n // tm,),
        in_specs=[pl.BlockSpec((tm, w), lambda i: (i, KA_OFF // w)),
                  pl.BlockSpec((tm, LANES), lambda i: (i, TAIL_OFF // LANES)),
                  pl.BlockSpec((1, HD_A), lambda i: (0, 0)),
                  pl.BlockSpec((1, D_IDX), lambda i: (0, 0))],
        out_specs=[pl.BlockSpec((tm, H_KV * HD_A), lambda i: (i, 0)),
                   pl.BlockSpec((tm, H_KV * HD_A), lambda i: (i, 0)),
                   pl.BlockSpec((tm, D_IDX), lambda i: (i, 0))],
        out_shape=[jax.ShapeDtypeStruct((n, H_KV * HD_A), F32),
                   jax.ShapeDtypeStruct((n, H_KV * HD_A), F32),
                   jax.ShapeDtypeStruct((n, D_IDX), F32)],
        compiler_params=_params("parallel"),
        name="kv_post",
    )(p, p, k_g, ik_g)


def _to_key(x):
    x = jnp.where(x == 0.0, 0.0, x)
    b = lax.bitcast_convert_type(x, jnp.int32)
    return jnp.where(b < 0, b ^ 0x7FFFFFFF, b)


def _dsa_kernel(qa_ref, qi_ref, tail_ref, k_ref, v_ref, ki_ref, qg_ref, o_ref,
                key_ref, m_ref, acc_ref, *, tq, tk, s_valid, past, top_k):
    i = pl.program_id(1)
    q0 = past + i * tq
    lim = jnp.minimum((((q0 + tq - 1) >> CHUNK_SHIFT) + 1) << CHUNK_SHIFT, s_valid)
    n_tiles = (lim + tk - 1) // tk
    lane = lax.broadcasted_iota(jnp.int32, (tq, LANES), 1)
    q_chunk = (q0 + lax.broadcasted_iota(jnp.int32, (tq, 1), 0)) >> CHUNK_SHIFT

    qi = qi_ref[...]
    qi_h = []
    for j in range(H_IDX // 2):
        pair = qi[:, j * LANES:(j + 1) * LANES] * (D_IDX ** -0.5)
        qi_h.append(jnp.where(lane < D_IDX, pair, 0.0).astype(BF16))
        qi_h.append(jnp.where(lane >= D_IDX, pair, 0.0).astype(BF16))
    wi = tail_ref[:, TAIL_WI:TAIL_WI + H_IDX] * (H_IDX ** -0.5)

    def tile_off(j):
        return pl.multiple_of(j * tk, tk)

    def score_tile(j, carry):
        off = tile_off(j)
        kt = ki_ref[pl.ds(off, tk), :]
        acc = jnp.zeros((tq, tk), F32)
        for h in range(H_IDX):
            acc = acc + jnp.maximum(_dot_nt(qi_h[h], kt), 0.0) * wi[:, h:h + 1]
        kpos = off + lax.broadcasted_iota(jnp.int32, (1, tk), 1)
        adm = ((kpos >> CHUNK_SHIFT) <= q_chunk) & (kpos < s_valid)
        key_ref[:, pl.ds(off, tk)] = _to_key(jnp.where(adm, acc, NEG_INF))
        return carry

    lax.fori_loop(0, n_tiles, score_tile, 0)

    def count(pred):
        def body(j, c):
            m = jnp.where(pred(key_ref[:, pl.ds(tile_off(j), tk)]), 1.0, 0.0)
            part = m[:, 0:LANES]
            for u in range(1, tk // LANES):
                part = part + m[:, u * LANES:(u + 1) * LANES]
            return c + part
        c = lax.fori_loop(0, n_tiles, body, jnp.zeros((tq, LANES), F32))
        return jnp.sum(c, axis=-1, keepdims=True)

    def bit_step(it, v):
        trial = v ^ lax.shift_left(jnp.int32(1), 31 - it)
        return jnp.where(count(lambda kt: kt >= trial) >= float(top_k), trial, v)

    v = lax.fori_loop(0, 32, bit_step, jnp.full((tq, 1), INT_MIN, jnp.int32))

    excess = (count(lambda kt: kt >= v) > float(top_k)) & (v > _KEY_VALID)

    @pl.when(jnp.max(jnp.where(excess, 1.0, 0.0)) > 0.0)
    def _():
        need = float(top_k) - count(lambda kt: kt > v)
        r_i = lax.broadcasted_iota(jnp.int32, (tk, tk), 0)
        c_i = lax.broadcasted_iota(jnp.int32, (tk, tk), 1)
        before = jnp.where(r_i < c_i, 1.0, 0.0).astype(BF16)

        def body(j, seen):
            off = tile_off(j)
            kt = key_ref[:, pl.ds(off, tk)]
            eq = kt == v
            eqf = jnp.where(eq, 1.0, 0.0)
            rank = seen + _dot(eqf.astype(BF16), before)
            key_ref[:, pl.ds(off, tk)] = jnp.where(eq & (rank >= need), v - 1, kt)
            return seen + jnp.sum(eqf, axis=-1, keepdims=True)

        lax.fori_loop(0, n_tiles, body, jnp.zeros((tq, 1), F32))

    qa = qa_ref[...]
    rep = H_A // H_KV
    qs = []
    for g in range(H_KV):
        heads = [(_rms(qa[:, h * HD_A:(h + 1) * HD_A], qg_ref[...]) * (HD_A ** -0.5 * LOG2E)).astype(BF16)
                 for h in range(g * rep, (g + 1) * rep)]
        qs.append(jnp.concatenate(heads, axis=0))
    m_ref[...] = jnp.full(m_ref.shape, NEG_INF, F32)
    acc_ref[...] = jnp.zeros(acc_ref.shape, F32)

    def att_tile(j, carry):
        off = tile_off(j)
        kt_key = key_ref[:, pl.ds(off, tk)]
        bias = jnp.where((kt_key >= v) & (kt_key > _KEY_VALID), 0.0, NEG_INF)
        logits = [_dot_nt(qs[g], k_ref[pl.ds(off, tk), g * HD_A:(g + 1) * HD_A]) for g in range(H_KV)]
        for g in range(H_KV):
            vt = v_ref[pl.ds(off, tk), 2 * g * HD_A:2 * (g + 1) * HD_A]
            s = (logits[g].reshape(rep, tq, tk) + bias[None]).reshape(rep * tq, tk)
            m_old = m_ref[g]
            m_new = jnp.maximum(m_old, jnp.max(s, axis=-1, keepdims=True))
            m_sub = jnp.where(m_new > NEG_INF * 0.5, m_new, -NEG_INF)
            p = jnp.concatenate([jnp.exp2(s[:, u * LANES:(u + 1) * LANES] - m_sub)
                                 for u in range(tk // LANES)], axis=1)
            alpha = jnp.exp2(m_old - m_new)
            pv = _dot(p.astype(BF16), vt)
            for u in range(2):
                cols = slice(u * HD_A, (u + 1) * HD_A)
                acc_ref[g, :, cols] = alpha * acc_ref[g, :, cols] + pv[:, cols]
            m_ref[g] = m_new
        return carry

    lax.fori_loop(0, n_tiles, att_tile, 0)
    for g in range(H_KV):
        out = acc_ref[g, :, 0:HD_A] / acc_ref[g, :, HD_A:2 * HD_A]
        for r in range(rep):
            h = g * rep + r
            o_ref[:, h * HD_A:(h + 1) * HD_A] = out[r * tq:(r + 1) * tq].astype(o_ref.dtype)


def _dsa(p, k_all, v_all, ki2, q_g, *, b, t, tq, tk, s_valid, past, top_k):
    nq = t // tq
    s_pad = k_all.shape[1]
    wq = H_A * HD_A
    kern = functools.partial(_dsa_kernel, tq=tq, tk=tk, s_valid=s_valid, past=past, top_k=top_k)
    return pl.pallas_call(
        kern,
        grid=(b, nq),
        in_specs=[pl.BlockSpec((tq, wq), lambda bb, i: (bb * nq + i, QA_OFF // wq)),
                  pl.BlockSpec((tq, wq), lambda bb, i: (bb * nq + i, QI_OFF // wq)),
                  pl.BlockSpec((tq, LANES), lambda bb, i: (bb * nq + i, TAIL_OFF // LANES)),
                  pl.BlockSpec((None, s_pad, H_KV * HD_A), lambda bb, i: (bb, 0, 0)),
                  pl.BlockSpec((None, s_pad, 2 * H_KV * HD_A), lambda bb, i: (bb, 0, 0)),
                  pl.BlockSpec((None, s_pad, LANES), lambda bb, i: (bb, 0, 0)),
                  pl.BlockSpec((1, HD_A), lambda bb, i: (0, 0))],
        out_specs=pl.BlockSpec((tq, wq), lambda bb, i: (bb * nq + i, 0)),
        out_shape=jax.ShapeDtypeStruct((b * t, wq), BF16),
        scratch_shapes=[pltpu.VMEM((tq, s_pad), jnp.int32),
                        pltpu.VMEM((H_KV, H_A // H_KV * tq, LANES), F32),
                        pltpu.VMEM((H_KV, H_A // H_KV * tq, 2 * HD_A), F32)],
        compiler_params=_params("parallel", "arbitrary"),
        name="dsa",
    )(p, p, p, k_all, v_all, ki2, q_g)


def _gla_kernel(q_ref, k_ref, v_ref, go_ref, tail_ref, wup_ref, bup_ref, ng_ref, s0_ref,
                o_ref, s_ref, la_ref, *, tt, chunk):
    @pl.when(pl.program_id(1) == 0)
    def _():
        s_ref[...] = s0_ref[...]

    glr = tail_ref[:, TAIL_GLR:TAIL_GLR + GATE_RANK].astype(BF16)
    z = _dot(glr, wup_ref[...]) + bup_ref[...]
    la_ref[...] = (jnp.minimum(z, 0.0) - jnp.log1p(jnp.exp(-jnp.abs(z)))) / GATE_TEMP

    width = H_G * DK_G
    row = lax.broadcasted_iota(jnp.int32, (chunk, width), 0)
    causal = (lax.broadcasted_iota(jnp.int32, (chunk, chunk), 0)
              >= lax.broadcasted_iota(jnp.int32, (chunk, chunk), 1))
    eye = (lax.broadcasted_iota(jnp.int32, (DK_G, DK_G), 0)
           == lax.broadcasted_iota(jnp.int32, (DK_G, DK_G), 1))

    def body(c, carry):
        off = pl.multiple_of(c * chunk, chunk)
        rows = pl.ds(off, chunk)
        bcum = la_ref[rows, :]
        step = 1
        while step < chunk:
            bcum = bcum + jnp.where(row >= step, pltpu.roll(bcum, step, axis=0), 0.0)
            step *= 2
        b_last = bcum[chunk - 1:chunk, :]
        q = q_ref[rows, :] * (DK_G ** -0.5)
        k = k_ref[rows, :]
        qe = (q * jnp.exp(bcum)).astype(BF16)
        ke = (k * jnp.exp(-bcum)).astype(BF16)
        kd = (k * jnp.exp(b_last - bcum)).astype(BF16)
        decay = jnp.exp(b_last)
        hk = [slice(h * DK_G, (h + 1) * DK_G) for h in range(H_G)]
        hv = [slice(h * DV_G, (h + 1) * DV_G) for h in range(H_G)]
        att = [jnp.where(causal, _dot_nt(qe[:, hk[h]], ke[:, hk[h]]), 0.0).astype(BF16)
               for h in range(H_G)]
        vv = [v_ref[rows, hv[h]].astype(BF16) for h in range(H_G)]
        state = [s_ref[h] for h in range(H_G)]
        inter = [_dot(qe[:, hk[h]], state[h].astype(BF16)) for h in range(H_G)]
        intra = [_dot(att[h], vv[h]) for h in range(H_G)]
        upd = [_dot_tn(kd[:, hk[h]], vv[h]) for h in range(H_G)]
        for h in range(H_G):
            decay_col = jnp.sum(jnp.where(eye, jnp.broadcast_to(decay[:, hk[h]], (DK_G, DK_G)), 0.0),
                                axis=1, keepdims=True)
            s_ref[h] = state[h] * decay_col + upd[h]
            go = go_ref[rows, hv[h]]
            o_ref[rows, hv[h]] = (_rms(intra[h] + inter[h], ng_ref[...])
                                  * (go * jax.nn.sigmoid(go))).astype(o_ref.dtype)
        return carry

    lax.fori_loop(0, tt // chunk, body, 0)


def _gla(p, w_up, b_up, n_g, s0, *, b, t, tt, chunk):
    nt = t // tt
    kern = functools.partial(_gla_kernel, tt=tt, chunk=chunk)
    wk, wv = H_G * DK_G, H_G * DV_G
    rowblk = lambda w, off: pl.BlockSpec((tt, w), lambda bb, i: (bb * nt + i, off // w))
    state = pl.BlockSpec((None, H_G, DK_G, DV_G), lambda bb, i: (bb, 0, 0, 0))
    return pl.pallas_call(
        kern,
        grid=(b, nt),
        in_specs=[rowblk(wk, QG_OFF), rowblk(wk, KG_OFF), rowblk(wv, VG_OFF), rowblk(wv, GO_OFF),
                  rowblk(LANES, TAIL_OFF),
                  pl.BlockSpec((GATE_RANK, wk), lambda bb, i: (0, 0)),
                  pl.BlockSpec((1, wk), lambda bb, i: (0, 0)),
                  pl.BlockSpec((1, DV_G), lambda bb, i: (0, 0)),
                  state],
        out_specs=[pl.BlockSpec((tt, wv), lambda bb, i: (bb * nt + i, 0)), state],
        out_shape=[jax.ShapeDtypeStruct((b * t, wv), BF16),
                   jax.ShapeDtypeStruct((b, H_G, DK_G, DV_G), F32)],
        scratch_shapes=[pltpu.VMEM((tt, wk), F32)],
        compiler_params=_params("parallel", "arbitrary"),
        name="gla",
    )(p, p, p, p, p, w_up, b_up, n_g, s0)


def _mem_kv_kernel(x_ref, g_ref, w_ref, kg_ref, mk_ref, mv_ref):
    kv = _dot(_rms(x_ref[...], g_ref[...]).astype(BF16), w_ref[...])
    for h in range(H_M):
        mk_ref[:, h * HD_M:(h + 1) * HD_M] = _rms(kv[:, h * HD_M:(h + 1) * HD_M], kg_ref[...])
    mv_ref[...] = kv[:, H_M * HD_M:]


def _mem_kv(mem, g, w, k_g, tm):
    n, d = mem.shape
    wk = H_M * HD_M
    return pl.pallas_call(
        _mem_kv_kernel,
        grid=(n // tm,),
        in_specs=[pl.BlockSpec((tm, d), lambda i: (i, 0)),
                  pl.BlockSpec((1, d), lambda i: (0, 0)),
                  pl.BlockSpec((d, 2 * wk), lambda i: (0, 0)),
                  pl.BlockSpec((1, HD_M), lambda i: (0, 0))],
        out_specs=[pl.BlockSpec((tm, wk), lambda i: (i, 0)),
                   pl.BlockSpec((tm, wk), lambda i: (i, 0))],
        out_shape=[jax.ShapeDtypeStruct((n, wk), F32), jax.ShapeDtypeStruct((n, wk), F32)],
        compiler_params=_params("parallel"),
        name="mem_kv",
    )(mem, g, w, k_g)


def _mem_attn_kernel(q_ref, mk_ref, mv_ref, g_ref, o_ref):
    qm = q_ref[...]
    for h in range(H_M):
        cols = slice(h * HD_M, (h + 1) * HD_M)
        q = (_rms(qm[:, cols], g_ref[...]) * (HD_M ** -0.5)).astype(BF16)
        s = _dot_nt(q, mk_ref[:, cols])
        e = jnp.exp(s - jnp.max(s, axis=-1, keepdims=True))
        prob = e / jnp.sum(e, axis=-1, keepdims=True)
        o_ref[:, cols] = _dot(prob.astype(BF16), mv_ref[:, cols]).astype(o_ref.dtype)


def _mem_attn(p, mk, mv, q_g, *, b, t, tt):
    nt = t // tt
    wq = H_M * HD_M
    n_mem = mk.shape[1]
    return pl.pallas_call(
        _mem_attn_kernel,
        grid=(b, nt),
        in_specs=[pl.BlockSpec((tt, wq), lambda bb, i: (bb * nt + i, QM_OFF // wq)),
                  pl.BlockSpec((None, n_mem, wq), lambda bb, i: (bb, 0, 0)),
                  pl.BlockSpec((None, n_mem, wq), lambda bb, i: (bb, 0, 0)),
                  pl.BlockSpec((1, HD_M), lambda bb, i: (0, 0))],
        out_specs=pl.BlockSpec((tt, wq), lambda bb, i: (bb * nt + i, 0)),
        out_shape=jax.ShapeDtypeStruct((b * t, wq), BF16),
        compiler_params=_params("parallel", "arbitrary"),
        name="mem_attn",
    )(p, mk, mv, q_g)


def _merge_kernel(x_ref, oa_ref, og_ref, om_ref, gb_ref, wa_ref, wg_ref, wm_ref, wo_ref,
                  fg_ref, wr_ref, br_ref, x1_ref, h2_ref, cb_ref):
    d = D_MODEL
    merged = (jax.nn.sigmoid(gb_ref[:, 0:d]) * _dot(oa_ref[...], wa_ref[...])
              + jax.nn.sigmoid(gb_ref[:, d:2 * d]) * _dot(og_ref[...], wg_ref[...])
              + jax.nn.sigmoid(gb_ref[:, 2 * d:3 * d]) * _dot(om_ref[...], wm_ref[...]))
    x1 = x_ref[...] + _dot(merged.astype(BF16), wo_ref[...])
    x1_ref[...] = x1
    hb = _rms(x1, fg_ref[...]).astype(BF16)
    h2_ref[...] = hb

    lg = _dot(hb, wr_ref[...]) + br_ref[...]
    lane = lax.broadcasted_iota(jnp.int32, lg.shape, 1).astype(F32)
    big = 1e9
    gmask = (lane >= N_EXPERTS) & (lane < N_EXPERTS + N_GROUPS)
    gl = jnp.where(gmask, lg, NEG_INF)
    gmax = jnp.max(gl, axis=-1, keepdims=True)
    gidx = jnp.min(jnp.where(gmask & (gl == gmax), lane, big), axis=-1, keepdims=True) - N_EXPERTS
    g_w = 1.0 / jnp.sum(jnp.where(gmask, jnp.exp(gl - gmax), 0.0), axis=-1, keepdims=True)
    lo = gidx * E_PER_GROUP
    emask = (lane >= lo) & (lane < lo + E_PER_GROUP)
    el = jnp.where(emask, lg, NEG_INF)
    ee = jnp.where(emask, jnp.exp(el - jnp.max(el, axis=-1, keepdims=True)), 0.0)
    ep = ee / jnp.sum(ee, axis=-1, keepdims=True)
    p1 = jnp.max(jnp.where(emask, ep, -1.0), axis=-1, keepdims=True)
    i1 = jnp.min(jnp.where(emask & (ep == p1), lane, big), axis=-1, keepdims=True)
    rest = emask & (lane != i1)
    p2 = jnp.max(jnp.where(rest, ep, -1.0), axis=-1, keepdims=True)
    i2 = jnp.min(jnp.where(rest & (ep == p2), lane, big), axis=-1, keepdims=True)
    den = p1 + p2
    cb_ref[...] = (jnp.where(lane == i1, g_w * (p1 / den), 0.0)
                   + jnp.where(lane == i2, g_w * (p2 / den), 0.0))


def _merge(x, oa, og, om, p, wa, wg, wm, wo, f_g, w_r, b_r, tm):
    n, d = x.shape
    full = lambda shape: pl.BlockSpec(shape, lambda i: (0, 0))
    rowblk = lambda w: pl.BlockSpec((tm, w), lambda i: (i, 0))
    return pl.pallas_call(
        _merge_kernel,
        grid=(n // tm,),
        in_specs=[rowblk(d), rowblk(d), rowblk(d), rowblk(d),
                  pl.BlockSpec((tm, 3 * d), lambda i: (i, GB_OFF // (3 * d))),
                  full((d, d)), full((d, d)), full((d, d)), full((d, d)),
                  full((1, d)), full((d, LANES)), full((1, LANES))],
        out_specs=[rowblk(d), rowblk(d), rowblk(LANES)],
        out_shape=[jax.ShapeDtypeStruct((n, d), F32),
                   jax.ShapeDtypeStruct((n, d), BF16),
                   jax.ShapeDtypeStruct((n, LANES), F32)],
        compiler_params=_params("parallel"),
        name="merge",
    )(x, oa, og, om, p, wa, wg, wm, wo, f_g, w_r, b_r)


def _moe_kernel(h_ref, cb_ref, x1_ref, wg_ref, wu_ref, wd_ref, y_ref):
    e = pl.program_id(1)

    @pl.when(e == 0)
    def _():
        y_ref[...] = x1_ref[...]

    h = h_ref[...]
    gate = _dot(h, wg_ref[...])
    up = _dot(h, wu_ref[...])
    cb = cb_ref[...]
    lane = lax.broadcasted_iota(jnp.int32, cb.shape, 1)
    c = jnp.sum(jnp.where(lane == e, cb, 0.0), axis=-1, keepdims=True)
    hid = (gate * jax.nn.sigmoid(gate)) * up * c
    y_ref[...] += _dot(hid.astype(BF16), wd_ref[...])


def _moe(h2, cb, x1, wg, wu, wd, tm):
    n, d = x1.shape
    return pl.pallas_call(
        _moe_kernel,
        grid=(n // tm, N_EXPERTS),
        in_specs=[pl.BlockSpec((tm, d), lambda i, e: (i, 0)),
                  pl.BlockSpec((tm, LANES), lambda i, e: (i, 0)),
                  pl.BlockSpec((tm, d), lambda i, e: (i, 0)),
                  pl.BlockSpec((None, d, D_EXPERT), lambda i, e: (e, 0, 0)),
                  pl.BlockSpec((None, d, D_EXPERT), lambda i, e: (e, 0, 0)),
                  pl.BlockSpec((None, D_EXPERT, d), lambda i, e: (e, 0, 0))],
        out_specs=pl.BlockSpec((tm, d), lambda i, e: (i, 0)),
        out_shape=jax.ShapeDtypeStruct((n, d), F32),
        compiler_params=_params("parallel", "arbitrary"),
        name="moe",
    )(h2, cb, x1, wg, wu, wd)


def _round_up(n, m):
    return (n + m - 1) // m * m


def _layer(x, mk, mv, past_k, past_v, past_ki, s0, w):
    b, t, d = x.shape
    n = b * t
    x2 = x.reshape(n, d)
    p = _norm_proj(x2, w["attn_g"], w["w_cat"], min(1024, n), 1024)
    ka, va, ki = _kv_post(p, w["k_g"], w["ik_g"], min(512, n))

    k3 = ka.reshape(b, t, H_KV * HD_A)
    v3 = va.reshape(b, t, H_KV * HD_A)
    ki3 = ki.reshape(b, t, D_IDX)
    past = 0
    if past_k is not None:
        past = past_k.shape[1]
        k3 = jnp.concatenate([past_k.reshape(b, past, -1), k3], axis=1)
        v3 = jnp.concatenate([past_v.reshape(b, past, -1), v3], axis=1)
        ki3 = jnp.concatenate([past_ki, ki3], axis=1)
    s_valid = past + t
    tk = 512
    pad = ((0, 0), (0, _round_up(s_valid, tk) - s_valid), (0, 0))
    k_all = jnp.pad(k3.astype(BF16), pad)
    ones = jnp.ones((b, s_valid, HD_A), BF16)
    v_parts = []
    for g in range(H_KV):
        v_parts += [v3[..., g * HD_A:(g + 1) * HD_A].astype(BF16), ones]
    v_all = jnp.pad(jnp.concatenate(v_parts, axis=-1), pad)
    ki2 = jnp.pad(jnp.concatenate([ki3, ki3], axis=-1).astype(BF16), pad)
    oa = _dsa(p, k_all, v_all, ki2, w["q_g"], b=b, t=t, tq=min(128, t), tk=tk,
              s_valid=s_valid, past=past, top_k=min(TOPK_MAX, s_valid // 4))

    chunk = CHUNK if t % CHUNK == 0 else t
    tt = min(512, t)
    og, s_new = _gla(p, w["w_up"], w["b_up"], w["gla_g"], s0, b=b, t=t, tt=tt, chunk=chunk)
    om = _mem_attn(p, mk, mv, w["mq_g"], b=b, t=t, tt=tt)
    x1, h2, cb = _merge(x2, oa, og, om, p, w["w_a"], w["w_g"], w["w_m"], w["w_o"],
                        w["ffn_g"], w["w_r"], w["b_r"], min(256, n))
    y = _moe(h2, cb, x1, w["w_eg"], w["w_eu"], w["w_ed"], min(1024, n))
    return (y.reshape(b, t, d), ka.reshape(b, t, H_KV, HD_A), va.reshape(b, t, H_KV, HD_A),
            ki.reshape(b, t, D_IDX), s_new)


def _prep_weights(l, attn_norm_g, w_in, q_norm_g, k_norm_g, idx_k_norm_g, w_gla_gate_up,
                  b_gla_gate_up, gla_out_norm_g, mem_q_norm_g, w_branch_a, w_branch_g,
                  w_branch_m, w_out, ffn_norm_g, w_group_router, b_group_router,
                  w_expert_router, b_expert_router, w_exp_gate, w_exp_up, w_exp_down):
    d = D_MODEL
    wi = w_in[l]
    cols = [wi[:, _SRC[name][0]:_SRC[name][1]] for name in _DST_ORDER]
    used = sum(c.shape[1] for c in cols)
    w_cat = jnp.concatenate(cols + [jnp.zeros((d, C_PAD - used), F32)], axis=1).astype(BF16)
    pad_r = LANES - N_EXPERTS - N_GROUPS
    w_r = jnp.concatenate([w_expert_router[l], w_group_router[l], jnp.zeros((d, pad_r), F32)], axis=1)
    b_r = jnp.concatenate([b_expert_router[l], b_group_router[l], jnp.zeros((pad_r,), F32)])
    return dict(
        attn_g=attn_norm_g[l][None], w_cat=w_cat, q_g=q_norm_g[l][None], k_g=k_norm_g[l][None],
        ik_g=idx_k_norm_g[l][None],
        w_up=w_gla_gate_up[l].astype(BF16), b_up=b_gla_gate_up[l][None],
        gla_g=gla_out_norm_g[l][None],
        mq_g=mem_q_norm_g[l][None],
        w_a=w_branch_a[l].astype(BF16), w_g=w_branch_g[l].astype(BF16),
        w_m=w_branch_m[l].astype(BF16), w_o=w_out[l].astype(BF16),
        ffn_g=ffn_norm_g[l][None], w_r=w_r.astype(BF16), b_r=b_r[None],
        w_eg=w_exp_gate[l].astype(BF16), w_eu=w_exp_up[l].astype(BF16),
        w_ed=w_exp_down[l].astype(BF16))


def kernel(x_prompt, x_sample, mem_prompt, cache_k, cache_v, cache_idx_k, state_gla, cache_mem_k, cache_mem_v, attn_norm_g, w_in, q_norm_g, k_norm_g, idx_k_norm_g, w_gla_gate_up, b_gla_gate_up, gla_out_norm_g, mem_norm_g, w_mem_kv, mem_q_norm_g, mem_k_norm_g, w_branch_a, w_branch_g, w_branch_m, w_out, ffn_norm_g, w_group_router, b_group_router, w_expert_router, b_expert_router, w_exp_gate, w_exp_up, w_exp_down):
    depth = w_in.shape[0]
    y_p, y_s = x_prompt, x_sample
    outs = [[] for _ in range(10)]
    for l in range(depth):
        w = _prep_weights(l, attn_norm_g, w_in, q_norm_g, k_norm_g, idx_k_norm_g, w_gla_gate_up,
                          b_gla_gate_up, gla_out_norm_g, mem_q_norm_g, w_branch_a, w_branch_g,
                          w_branch_m, w_out, ffn_norm_g, w_group_router, b_group_router,
                          w_expert_router, b_expert_router, w_exp_gate, w_exp_up, w_exp_down)
        bp, n_mem, d = mem_prompt.shape
        mk_p, mv_p = _mem_kv(mem_prompt.reshape(bp * n_mem, d), mem_norm_g[l][None],
                             w_mem_kv[l].astype(BF16), mem_k_norm_g[l][None], min(256, bp * n_mem))
        mk_p = mk_p.reshape(bp, n_mem, H_M, HD_M)
        mv_p = mv_p.reshape(bp, n_mem, H_M, HD_M)
        s0 = jnp.zeros((bp, H_G, DK_G, DV_G), F32)
        y_p, k_new, v_new, ki_new, s_new = _layer(
            y_p, mk_p.reshape(bp, n_mem, -1).astype(BF16), mv_p.reshape(bp, n_mem, -1).astype(BF16),
            None, None, None, s0, w)
        for lst, val in zip(outs[:6], (k_new, v_new, ki_new, s_new, mk_p, mv_p)):
            lst.append(val)
        bs = x_sample.shape[0]
        y_s, k_new, v_new, ki_new, s_new = _layer(
            y_s, cache_mem_k[l].reshape(bs, n_mem, -1).astype(BF16),
            cache_mem_v[l].reshape(bs, n_mem, -1).astype(BF16),
            cache_k[l], cache_v[l], cache_idx_k[l], state_gla[l], w)
        for lst, val in zip(outs[6:], (k_new, v_new, ki_new, s_new)):
            lst.append(val)
    return (y_p, y_s) + tuple(jnp.stack(o) for o in outs)
```

```python
import functools

import numpy as np
import jax
import jax.numpy as jnp
from jax import lax
from jax.experimental import pallas as pl
from jax.experimental.pallas import tpu as pltpu

F32 = jnp.float32
BF16 = jnp.bfloat16

D_MODEL = 1024
CHUNK = 64
CHUNK_SHIFT = 6
TOPK_MAX = 256
H_A, HD_A, H_KV = 8, 128, 2
H_IDX, D_IDX = 16, 64
H_G, DK_G, DV_G = 4, 128, 256
GATE_RANK, GATE_TEMP = 16, 16.0
H_M, HD_M = 4, 256
N_GROUPS, E_PER_GROUP = 4, 8
N_EXPERTS = N_GROUPS * E_PER_GROUP
D_EXPERT = 256
EPS = 1e-6
NEG_INF = -1e30
LANES = 128
LOG2E = 1.4426950408889634
GROUP_LANE = LANES - 1
COUNT_ROWS = 128
MOE_SUB = 128
MOE_E_STEP = 4

QA_OFF, QI_OFF, QG_OFF, KG_OFF, VG_OFF = 0, 1024, 2048, 2560, 3072
GO_OFF, QM_OFF, GB_OFF, KA_OFF, VA_OFF, TAIL_OFF = 4096, 5120, 6144, 9216, 9472, 9728
TAIL_KI, TAIL_WI, TAIL_GLR = 0, 64, 80
C_PAD = 10240
_SRC = dict(qa=(0, 1024), ka=(1024, 1280), va=(1280, 1536), qi=(1536, 2560), ki=(2560, 2624),
            wi=(2624, 2640), qg=(2640, 3152), kg=(3152, 3664), vg=(3664, 4688), glr=(4688, 4704),
            go=(4704, 5728), qm=(5728, 6752), gb=(6752, 9824))
_DST_ORDER = ("qa", "qi", "qg", "kg", "vg", "go", "qm", "gb", "ka", "va", "ki", "wi", "glr")

INT_MIN = -2 ** 31
_KEY_VALID = int(np.float32(NEG_INF * 0.5).view(np.int32)) ^ 0x7FFFFFFF
if _KEY_VALID >= 2 ** 31:
    _KEY_VALID -= 2 ** 32

VMEM_LIMIT = 56 * 1024 * 1024


def _params(*sem):
    return pltpu.CompilerParams(dimension_semantics=sem, vmem_limit_bytes=VMEM_LIMIT)


def _rms(x, g):
    return x * lax.rsqrt(jnp.mean(x * x, axis=-1, keepdims=True) + EPS) * g


def _dot(a, b):
    return jnp.dot(a, b, preferred_element_type=F32)


def _dot_nt(a, b):
    return lax.dot_general(a, b, (((1,), (1,)), ((), ())), preferred_element_type=F32)


def _dot_tn(a, b):
    return lax.dot_general(a, b, (((0,), (0,)), ((), ())), preferred_element_type=F32)


def _norm_proj_kernel(x_ref, g_ref, w_ref, o_ref, h_ref):
    @pl.when(pl.program_id(1) == 0)
    def _():
        h_ref[...] = _rms(x_ref[...], g_ref[...]).astype(BF16)

    o_ref[...] = _dot(h_ref[...], w_ref[...])


def _norm_proj(x, g, w, tm, tn):
    n, d = x.shape
    c = w.shape[1]
    return pl.pallas_call(
        _norm_proj_kernel,
        grid=(n // tm, c // tn),
        in_specs=[pl.BlockSpec((tm, d), lambda i, j: (i, 0)),
                  pl.BlockSpec((1, d), lambda i, j: (0, 0)),
                  pl.BlockSpec((d, tn), lambda i, j: (0, j))],
        out_specs=pl.BlockSpec((tm, tn), lambda i, j: (i, j)),
        out_shape=jax.ShapeDtypeStruct((n, c), F32),
        scratch_shapes=[pltpu.VMEM((tm, d), BF16)],
        compiler_params=_params("parallel", "arbitrary"),
        name="norm_proj",
    )(x, g, w)


def _kv_post_kernel(kv_ref, tail_ref, kg_ref, ig_ref, k_ref, v_ref, ki_ref):
    kv = kv_ref[...]
    for h in range(H_KV):
        k_ref[:, h * HD_A:(h + 1) * HD_A] = _rms(kv[:, h * HD_A:(h + 1) * HD_A], kg_ref[...])
    v_ref[...] = kv[:, H_KV * HD_A:]
    ki_ref[...] = _rms(tail_ref[:, TAIL_KI:TAIL_KI + D_IDX], ig_ref[...])


def _kv_post(p, k_g, ik_g, tm):
    n = p.shape[0]
    w = 2 * H_KV * HD_A
    return pl.pallas_call(
        _kv_post_kernel,
        grid=(n // tm,),
        in_specs=[pl.BlockSpec((tm, w), lambda i: (i, KA_OFF // w)),
                  pl.BlockSpec((tm, LANES), lambda i: (i, TAIL_OFF // LANES)),
                  pl.BlockSpec((1, HD_A), lambda i: (0, 0)),
                  pl.BlockSpec((1, D_IDX), lambda i: (0, 0))],
        out_specs=[pl.BlockSpec((tm, H_KV * HD_A), lambda i: (i, 0)),
                   pl.BlockSpec((tm, H_KV * HD_A), lambda i: (i, 0)),
                   pl.BlockSpec((tm, D_IDX), lambda i: (i, 0))],
        out_shape=[jax.ShapeDtypeStruct((n, H_KV * HD_A), F32),
                   jax.ShapeDtypeStruct((n, H_KV * HD_A), F32),
                   jax.ShapeDtypeStruct((n, D_IDX), F32)],
        compiler_params=_params("parallel"),
        name="kv_post",
    )(p, p, k_g, ik_g)


def _to_key(x):
    x = jnp.where(x == 0.0, 0.0, x)
    b = lax.bitcast_convert_type(x, jnp.int32)
    return jnp.where(b < 0, b ^ 0x7FFFFFFF, b)


def _dsa_kernel(qa_ref, qi_ref, tail_ref, k_ref, v_ref, ki_ref, qg_ref, o_ref,
                key_ref, m_ref, acc_ref, *, tq, tk, s_valid, past, top_k):
    i = pl.program_id(1)
    q0 = past + i * tq
    lim = jnp.minimum((((q0 + tq - 1) >> CHUNK_SHIFT) + 1) << CHUNK_SHIFT, s_valid)
    n_tiles = (lim + tk - 1) // tk
    lane = lax.broadcasted_iota(jnp.int32, (tq, LANES), 1)
    q_chunk = (q0 + lax.broadcasted_iota(jnp.int32, (tq, 1), 0)) >> CHUNK_SHIFT

    qi = qi_ref[...]
    qi_h = []
    for j in range(H_IDX // 2):
        pair = qi[:, j * LANES:(j + 1) * LANES] * (D_IDX ** -0.5)
        qi_h.append(jnp.where(lane < D_IDX, pair, 0.0).astype(BF16))
        qi_h.append(jnp.where(lane >= D_IDX, pair, 0.0).astype(BF16))
    wi = tail_ref[:, TAIL_WI:TAIL_WI + H_IDX] * (H_IDX ** -0.5)

    def tile_off(j):
        return pl.multiple_of(j * tk, tk)

    def score_tile(j, carry):
        off = tile_off(j)
        kt = ki_ref[:, pl.ds(off, tk)]
        acc = jnp.zeros((tq, tk), F32)
        for h in range(H_IDX):
            acc = acc + jnp.maximum(_dot(qi_h[h], kt), 0.0) * wi[:, h:h + 1]
        kpos = off + lax.broadcasted_iota(jnp.int32, (1, tk), 1)
        adm = ((kpos >> CHUNK_SHIFT) <= q_chunk) & (kpos < s_valid)
        key_ref[:, pl.ds(off, tk)] = _to_key(jnp.where(adm, acc, NEG_INF))
        return carry

    lax.fori_loop(0, n_tiles, score_tile, 0)

    rb = min(tq, COUNT_ROWS)

    def count(cmp, thr):
        outs = []
        for r0 in range(0, tq, rb):
            th = thr[r0:r0 + rb]

            def body(j, c, r0=r0, th=th):
                m = jnp.where(cmp(key_ref[r0:r0 + rb, pl.ds(tile_off(j), tk)], th), 1.0, 0.0)
                part = m[:, 0:LANES]
                for u in range(1, tk // LANES):
                    part = part + m[:, u * LANES:(u + 1) * LANES]
                return c + part

            c = lax.fori_loop(0, n_tiles, body, jnp.zeros((rb, LANES), F32))
            outs.append(jnp.sum(c, axis=-1, keepdims=True))
        return outs[0] if len(outs) == 1 else jnp.concatenate(outs, axis=0)

    ge = lambda a, b: a >= b
    gt = lambda a, b: a > b

    def bit_step(it, v):
        trial = v ^ lax.shift_left(jnp.int32(1), 31 - it)
        return jnp.where(count(ge, trial) >= float(top_k), trial, v)

    v = lax.fori_loop(0, 32, bit_step, jnp.full((tq, 1), INT_MIN, jnp.int32))

    excess = (count(ge, v) > float(top_k)) & (v > _KEY_VALID)

    @pl.when(jnp.max(jnp.where(excess, 1.0, 0.0)) > 0.0)
    def _():
        need = float(top_k) - count(gt, v)
        r_i = lax.broadcasted_iota(jnp.int32, (tk, tk), 0)
        c_i = lax.broadcasted_iota(jnp.int32, (tk, tk), 1)
        before = jnp.where(r_i < c_i, 1.0, 0.0).astype(BF16)

        def body(j, seen):
            off = tile_off(j)
            kt = key_ref[:, pl.ds(off, tk)]
            eq = kt == v
            eqf = jnp.where(eq, 1.0, 0.0)
            rank = seen + _dot(eqf.astype(BF16), before)
            key_ref[:, pl.ds(off, tk)] = jnp.where(eq & (rank >= need), v - 1, kt)
            return seen + jnp.sum(eqf, axis=-1, keepdims=True)

        lax.fori_loop(0, n_tiles, body, jnp.zeros((tq, 1), F32))

    qa = qa_ref[...]
    rep = H_A // H_KV
    qs = []
    for g in range(H_KV):
        heads = [(_rms(qa[:, h * HD_A:(h + 1) * HD_A], qg_ref[...]) * (HD_A ** -0.5 * LOG2E)).astype(BF16)
                 for h in range(g * rep, (g + 1) * rep)]
        qs.append(jnp.concatenate(heads, axis=0))
    m_ref[...] = jnp.full(m_ref.shape, NEG_INF, F32)
    acc_ref[...] = jnp.zeros(acc_ref.shape, F32)

    def att_tile(j, carry):
        off = tile_off(j)
        kt_key = key_ref[:, pl.ds(off, tk)]
        bias = jnp.where((kt_key >= v) & (kt_key > _KEY_VALID), 0.0, NEG_INF)
        logits = [_dot(qs[g], k_ref[g, :, pl.ds(off, tk)]) for g in range(H_KV)]
        for g in range(H_KV):
            vt = v_ref[pl.ds(off, tk), 2 * g * HD_A:2 * (g + 1) * HD_A]
            s = (logits[g].reshape(rep, tq, tk) + bias[None]).reshape(rep * tq, tk)
            m_old = m_ref[g]
            m_new = jnp.maximum(m_old, jnp.max(s, axis=-1, keepdims=True))
            m_sub = jnp.where(m_new > NEG_INF * 0.5, m_new, -NEG_INF)
            p = jnp.concatenate([jnp.exp2(s[:, u * LANES:(u + 1) * LANES] - m_sub)
                                 for u in range(tk // LANES)], axis=1)
            alpha = jnp.exp2(m_old - m_new)
            pv = _dot(p.astype(BF16), vt)
            for u in range(2):
                cols = slice(u * HD_A, (u + 1) * HD_A)
                acc_ref[g, :, cols] = alpha * acc_ref[g, :, cols] + pv[:, cols]
            m_ref[g] = m_new
        return carry

    lax.fori_loop(0, n_tiles, att_tile, 0)
    for g in range(H_KV):
        out = acc_ref[g, :, 0:HD_A] / acc_ref[g, :, HD_A:2 * HD_A]
        for r in range(rep):
            h = g * rep + r
            o_ref[:, h * HD_A:(h + 1) * HD_A] = out[r * tq:(r + 1) * tq].astype(o_ref.dtype)


def _dsa(p, k_all, v_all, ki2, q_g, *, b, t, tq, tk, s_valid, past, top_k):
    nq = t // tq
    s_pad = k_all.shape[-1]
    wq = H_A * HD_A
    kern = functools.partial(_dsa_kernel, tq=tq, tk=tk, s_valid=s_valid, past=past, top_k=top_k)
    return pl.pallas_call(
        kern,
        grid=(b, nq),
        in_specs=[pl.BlockSpec((tq, wq), lambda bb, i: (bb * nq + i, QA_OFF // wq)),
                  pl.BlockSpec((tq, wq), lambda bb, i: (bb * nq + i, QI_OFF // wq)),
                  pl.BlockSpec((tq, LANES), lambda bb, i: (bb * nq + i, TAIL_OFF // LANES)),
                  pl.BlockSpec((None, H_KV, HD_A, s_pad), lambda bb, i: (bb, 0, 0, 0)),
                  pl.BlockSpec((None, s_pad, 2 * H_KV * HD_A), lambda bb, i: (bb, 0, 0)),
                  pl.BlockSpec((None, LANES, s_pad), lambda bb, i: (bb, 0, 0)),
                  pl.BlockSpec((1, HD_A), lambda bb, i: (0, 0))],
        out_specs=pl.BlockSpec((tq, wq), lambda bb, i: (bb * nq + i, 0)),
        out_shape=jax.ShapeDtypeStruct((b * t, wq), BF16),
        scratch_shapes=[pltpu.VMEM((tq, s_pad), jnp.int32),
                        pltpu.VMEM((H_KV, H_A // H_KV * tq, LANES), F32),
                        pltpu.VMEM((H_KV, H_A // H_KV * tq, 2 * HD_A), F32)],
        compiler_params=_params("parallel", "arbitrary"),
        name="dsa",
    )(p, p, p, k_all, v_all, ki2, q_g)


def _gla_kernel(q_ref, k_ref, v_ref, go_ref, tail_ref, wup_ref, bup_ref, ng_ref, s0_ref,
                o_ref, s_ref, la_ref, *, tt, chunk):
    @pl.when(pl.program_id(1) == 0)
    def _():
        s_ref[...] = s0_ref[...]

    glr = tail_ref[:, TAIL_GLR:TAIL_GLR + GATE_RANK].astype(BF16)
    z = _dot(glr, wup_ref[...]) + bup_ref[...]
    la_ref[...] = (jnp.minimum(z, 0.0) - jnp.log1p(jnp.exp(-jnp.abs(z)))) / GATE_TEMP

    width = H_G * DK_G
    row = lax.broadcasted_iota(jnp.int32, (chunk, width), 0)
    causal = (lax.broadcasted_iota(jnp.int32, (chunk, chunk), 0)
              >= lax.broadcasted_iota(jnp.int32, (chunk, chunk), 1))
    eye = (lax.broadcasted_iota(jnp.int32, (DK_G, DK_G), 0)
           == lax.broadcasted_iota(jnp.int32, (DK_G, DK_G), 1))

    def body(c, carry):
        off = pl.multiple_of(c * chunk, chunk)
        rows = pl.ds(off, chunk)
        bcum = la_ref[rows, :]
        step = 1
        while step < chunk:
            bcum = bcum + jnp.where(row >= step, pltpu.roll(bcum, step, axis=0), 0.0)
            step *= 2
        b_last = bcum[chunk - 1:chunk, :]
        q = q_ref[rows, :] * (DK_G ** -0.5)
        k = k_ref[rows, :]
        qe = (q * jnp.exp(bcum)).astype(BF16)
        ke = (k * jnp.exp(-bcum)).astype(BF16)
        kd = (k * jnp.exp(b_last - bcum)).astype(BF16)
        decay = jnp.exp(b_last)
        hk = [slice(h * DK_G, (h + 1) * DK_G) for h in range(H_G)]
        hv = [slice(h * DV_G, (h + 1) * DV_G) for h in range(H_G)]
        att = [jnp.where(causal, _dot_nt(qe[:, hk[h]], ke[:, hk[h]]), 0.0).astype(BF16)
               for h in range(H_G)]
        vv = [v_ref[rows, hv[h]].astype(BF16) for h in range(H_G)]
        state = [s_ref[h] for h in range(H_G)]
        inter = [_dot(qe[:, hk[h]], state[h].astype(BF16)) for h in range(H_G)]
        intra = [_dot(att[h], vv[h]) for h in range(H_G)]
        upd = [_dot_tn(kd[:, hk[h]], vv[h]) for h in range(H_G)]
        for h in range(H_G):
            decay_col = jnp.sum(jnp.where(eye, jnp.broadcast_to(decay[:, hk[h]], (DK_G, DK_G)), 0.0),
                                axis=1, keepdims=True)
            s_ref[h] = state[h] * decay_col + upd[h]
            go = go_ref[rows, hv[h]]
            o_ref[rows, hv[h]] = (_rms(intra[h] + inter[h], ng_ref[...])
                                  * (go * jax.nn.sigmoid(go))).astype(o_ref.dtype)
        return carry

    lax.fori_loop(0, tt // chunk, body, 0)


def _gla(p, w_up, b_up, n_g, s0, *, b, t, tt, chunk):
    nt = t // tt
    kern = functools.partial(_gla_kernel, tt=tt, chunk=chunk)
    wk, wv = H_G * DK_G, H_G * DV_G
    rowblk = lambda w, off: pl.BlockSpec((tt, w), lambda bb, i: (bb * nt + i, off // w))
    state = pl.BlockSpec((None, H_G, DK_G, DV_G), lambda bb, i: (bb, 0, 0, 0))
    return pl.pallas_call(
        kern,
        grid=(b, nt),
        in_specs=[rowblk(wk, QG_OFF), rowblk(wk, KG_OFF), rowblk(wv, VG_OFF), rowblk(wv, GO_OFF),
                  rowblk(LANES, TAIL_OFF),
                  pl.BlockSpec((GATE_RANK, wk), lambda bb, i: (0, 0)),
                  pl.BlockSpec((1, wk), lambda bb, i: (0, 0)),
                  pl.BlockSpec((1, DV_G), lambda bb, i: (0, 0)),
                  state],
        out_specs=[pl.BlockSpec((tt, wv), lambda bb, i: (bb * nt + i, 0)), state],
        out_shape=[jax.ShapeDtypeStruct((b * t, wv), BF16),
                   jax.ShapeDtypeStruct((b, H_G, DK_G, DV_G), F32)],
        scratch_shapes=[pltpu.VMEM((tt, wk), F32)],
        compiler_params=_params("parallel", "arbitrary"),
        name="gla",
    )(p, p, p, p, p, w_up, b_up, n_g, s0)


def _mem_kv_kernel(x_ref, g_ref, w_ref, kg_ref, mk_ref, mv_ref):
    kv = _dot(_rms(x_ref[...], g_ref[...]).astype(BF16), w_ref[...])
    for h in range(H_M):
        mk_ref[:, h * HD_M:(h + 1) * HD_M] = _rms(kv[:, h * HD_M:(h + 1) * HD_M], kg_ref[...])
    mv_ref[...] = kv[:, H_M * HD_M:]


def _mem_kv(mem, g, w, k_g, tm):
    n, d = mem.shape
    wk = H_M * HD_M
    return pl.pallas_call(
        _mem_kv_kernel,
        grid=(n // tm,),
        in_specs=[pl.BlockSpec((tm, d), lambda i: (i, 0)),
                  pl.BlockSpec((1, d), lambda i: (0, 0)),
                  pl.BlockSpec((d, 2 * wk), lambda i: (0, 0)),
                  pl.BlockSpec((1, HD_M), lambda i: (0, 0))],
        out_specs=[pl.BlockSpec((tm, wk), lambda i: (i, 0)),
                   pl.BlockSpec((tm, wk), lambda i: (i, 0))],
        out_shape=[jax.ShapeDtypeStruct((n, wk), F32), jax.ShapeDtypeStruct((n, wk), F32)],
        compiler_params=_params("parallel"),
        name="mem_kv",
    )(mem, g, w, k_g)


def _mem_attn_kernel(q_ref, mk_ref, mv_ref, g_ref, o_ref):
    qm = q_ref[...]
    for h in range(H_M):
        cols = slice(h * HD_M, (h + 1) * HD_M)
        q = (_rms(qm[:, cols], g_ref[...]) * (HD_M ** -0.5)).astype(BF16)
        s = _dot_nt(q, mk_ref[:, cols])
        e = jnp.exp(s - jnp.max(s, axis=-1, keepdims=True))
        prob = e / jnp.sum(e, axis=-1, keepdims=True)
        o_ref[:, cols] = _dot(prob.astype(BF16), mv_ref[:, cols]).astype(o_ref.dtype)


def _mem_attn(p, mk, mv, q_g, *, b, t, tt):
    nt = t // tt
    wq = H_M * HD_M
    n_mem = mk.shape[1]
    return pl.pallas_call(
        _mem_attn_kernel,
        grid=(b, nt),
        in_specs=[pl.BlockSpec((tt, wq), lambda bb, i: (bb * nt + i, QM_OFF // wq)),
                  pl.BlockSpec((None, n_mem, wq), lambda bb, i: (bb, 0, 0)),
                  pl.BlockSpec((None, n_mem, wq), lambda bb, i: (bb, 0, 0)),
                  pl.BlockSpec((1, HD_M), lambda bb, i: (0, 0))],
        out_specs=pl.BlockSpec((tt, wq), lambda bb, i: (bb * nt + i, 0)),
        out_shape=jax.ShapeDtypeStruct((b * t, wq), BF16),
        compiler_params=_params("parallel", "arbitrary"),
        name="mem_attn",
    )(p, mk, mv, q_g)


def _merge_kernel(x_ref, oa_ref, og_ref, om_ref, gb_ref, wa_ref, wg_ref, wm_ref, wo_ref,
                  fg_ref, wr_ref, br_ref, x1_ref, h2_ref, cb_ref):
    d = D_MODEL
    merged = (jax.nn.sigmoid(gb_ref[:, 0:d]) * _dot(oa_ref[...], wa_ref[...])
              + jax.nn.sigmoid(gb_ref[:, d:2 * d]) * _dot(og_ref[...], wg_ref[...])
              + jax.nn.sigmoid(gb_ref[:, 2 * d:3 * d]) * _dot(om_ref[...], wm_ref[...]))
    x1 = x_ref[...] + _dot(merged.astype(BF16), wo_ref[...])
    x1_ref[...] = x1
    hb = _rms(x1, fg_ref[...]).astype(BF16)
    h2_ref[...] = hb

    lg = _dot(hb, wr_ref[...]) + br_ref[...]
    lane = lax.broadcasted_iota(jnp.int32, lg.shape, 1).astype(F32)
    big = 1e9
    gmask = (lane >= N_EXPERTS) & (lane < N_EXPERTS + N_GROUPS)
    gl = jnp.where(gmask, lg, NEG_INF)
    gmax = jnp.max(gl, axis=-1, keepdims=True)
    gidx = jnp.min(jnp.where(gmask & (gl == gmax), lane, big), axis=-1, keepdims=True) - N_EXPERTS
    g_w = 1.0 / jnp.sum(jnp.where(gmask, jnp.exp(gl - gmax), 0.0), axis=-1, keepdims=True)
    lo = gidx * E_PER_GROUP
    emask = (lane >= lo) & (lane < lo + E_PER_GROUP)
    el = jnp.where(emask, lg, NEG_INF)
    ee = jnp.where(emask, jnp.exp(el - jnp.max(el, axis=-1, keepdims=True)), 0.0)
    ep = ee / jnp.sum(ee, axis=-1, keepdims=True)
    p1 = jnp.max(jnp.where(emask, ep, -1.0), axis=-1, keepdims=True)
    i1 = jnp.min(jnp.where(emask & (ep == p1), lane, big), axis=-1, keepdims=True)
    rest = emask & (lane != i1)
    p2 = jnp.max(jnp.where(rest, ep, -1.0), axis=-1, keepdims=True)
    i2 = jnp.min(jnp.where(rest & (ep == p2), lane, big), axis=-1, keepdims=True)
    den = p1 + p2
    cb_ref[...] = (jnp.where(lane == i1, g_w * (p1 / den), 0.0)
                   + jnp.where(lane == i2, g_w * (p2 / den), 0.0)
                   + jnp.where(lane == GROUP_LANE, gidx, 0.0))


def _merge(x, oa, og, om, p, wa, wg, wm, wo, f_g, w_r, b_r, tm):
    n, d = x.shape
    full = lambda shape: pl.BlockSpec(shape, lambda i: (0, 0))
    rowblk = lambda w: pl.BlockSpec((tm, w), lambda i: (i, 0))
    return pl.pallas_call(
        _merge_kernel,
        grid=(n // tm,),
        in_specs=[rowblk(d), rowblk(d), rowblk(d), rowblk(d),
                  pl.BlockSpec((tm, 3 * d), lambda i: (i, GB_OFF // (3 * d))),
                  full((d, d)), full((d, d)), full((d, d)), full((d, d)),
                  full((1, d)), full((d, LANES)), full((1, LANES))],
        out_specs=[rowblk(d), rowblk(d), rowblk(LANES)],
        out_shape=[jax.ShapeDtypeStruct((n, d), F32),
                   jax.ShapeDtypeStruct((n, d), BF16),
                   jax.ShapeDtypeStruct((n, LANES), F32)],
        compiler_params=_params("parallel"),
        name="merge",
    )(x, oa, og, om, p, wa, wg, wm, wo, f_g, w_r, b_r)


def _split3(x):
    hi = x.astype(BF16)
    r1 = x - hi.astype(F32)
    mid = r1.astype(BF16)
    lo = (r1 - mid.astype(F32)).astype(BF16)
    return hi, mid, lo


def _permute_rows(onehot, x):
    hi, mid, lo = _split3(x)
    return _dot(onehot, hi) + _dot(onehot, mid) + _dot(onehot, lo)


def _moe_kernel(h_ref, cb_ref, x1_ref, wg_ref, wu_ref, wd_ref, y_ref,
                hp_ref, cbp_ref, yp_ref, dest_ref, seg_ref):
    e = pl.program_id(1)
    tm = h_ref.shape[0]

    @pl.when(e == 0)
    def _():
        cb = cb_ref[...]
        g_row = cb.T[GROUP_LANE:GROUP_LANE + 1, :]
        grp = lax.broadcasted_iota(jnp.int32, (8, tm), 0).astype(F32)
        member = jnp.where(g_row == grp, 1.0, 0.0)
        r_i = lax.broadcasted_iota(jnp.int32, (tm, tm), 0)
        c_i = lax.broadcasted_iota(jnp.int32, (tm, tm), 1)
        upto = jnp.where(r_i <= c_i, 1.0, 0.0).astype(BF16)
        cnt = _dot(member.astype(BF16), upto)
        tot = cnt[:, tm - 1:tm]
        start = jnp.zeros((1, 1), F32)
        dest = jnp.zeros((1, tm), F32)
        seg_ref[0] = 0
        for j in range(N_GROUPS):
            dest = dest + member[j:j + 1] * (start + cnt[j:j + 1] - 1.0)
            start = start + tot[j:j + 1]
            seg_ref[j + 1] = jnp.sum(start).astype(jnp.int32)
        place = jnp.where(r_i.astype(F32) == dest, 1.0, 0.0).astype(BF16)
        hp_ref[...] = _dot(place, h_ref[...]).astype(BF16)
        cbp_ref[...] = _permute_rows(place, cb)
        dest_ref[...] = jnp.broadcast_to(dest, (LANES, tm)).T
        yp_ref[...] = jnp.zeros(yp_ref.shape, F32)

    sub = min(MOE_SUB, tm)
    n_e = wg_ref.shape[0]
    g = (e * n_e) // E_PER_GROUP
    first = seg_ref[g] // sub
    last = (seg_ref[g + 1] + sub - 1) // sub
    lane = lax.broadcasted_iota(jnp.int32, (sub, LANES), 1)

    def sub_tile(s, carry):
        rows = pl.ds(pl.multiple_of(s * sub, sub), sub)
        h = hp_ref[rows, :]
        cbp = cbp_ref[rows, :]
        gate = [_dot(h, wg_ref[k]) for k in range(n_e)]
        up = [_dot(h, wu_ref[k]) for k in range(n_e)]
        out = None
        for k in range(n_e):
            c = jnp.sum(jnp.where(lane == e * n_e + k, cbp, 0.0), axis=-1, keepdims=True)
            hid = (gate[k] * jax.nn.sigmoid(gate[k])) * up[k] * c
            part = _dot(hid.astype(BF16), wd_ref[k])
            out = part if out is None else out + part
        yp_ref[rows, :] += out
        return carry

    lax.fori_loop(first, last, sub_tile, 0)

    @pl.when(e == pl.num_programs(1) - 1)
    def _():
        col = lax.broadcasted_iota(jnp.int32, (tm, tm), 1).astype(F32)
        back = jnp.where(col == dest_ref[:, 0:1], 1.0, 0.0).astype(BF16)
        y_ref[...] = x1_ref[...] + _permute_rows(back, yp_ref[...])


def _moe(h2, cb, x1, wg, wu, wd, tm):
    n, d = x1.shape
    return pl.pallas_call(
        _moe_kernel,
        grid=(n // tm, N_EXPERTS // MOE_E_STEP),
        in_specs=[pl.BlockSpec((tm, d), lambda i, e: (i, 0)),
                  pl.BlockSpec((tm, LANES), lambda i, e: (i, 0)),
                  pl.BlockSpec((tm, d), lambda i, e: (i, 0)),
                  pl.BlockSpec((MOE_E_STEP, d, D_EXPERT), lambda i, e: (e, 0, 0)),
                  pl.BlockSpec((MOE_E_STEP, d, D_EXPERT), lambda i, e: (e, 0, 0)),
                  pl.BlockSpec((MOE_E_STEP, D_EXPERT, d), lambda i, e: (e, 0, 0))],
        out_specs=pl.BlockSpec((tm, d), lambda i, e: (i, 0)),
        out_shape=jax.ShapeDtypeStruct((n, d), F32),
        scratch_shapes=[pltpu.VMEM((tm, d), BF16),
                        pltpu.VMEM((tm, LANES), F32),
                        pltpu.VMEM((tm, d), F32),
                        pltpu.VMEM((tm, LANES), F32),
                        pltpu.SMEM((8,), jnp.int32)],
        compiler_params=_params("parallel", "arbitrary"),
        name="moe",
    )(h2, cb, x1, wg, wu, wd)


def _round_up(n, m):
    return (n + m - 1) // m * m


def _layer(x, mk, mv, past_k, past_v, past_ki, s0, w):
    b, t, d = x.shape
    n = b * t
    x2 = x.reshape(n, d)
    p = _norm_proj(x2, w["attn_g"], w["w_cat"], min(1024, n), 1024)
    ka, va, ki = _kv_post(p, w["k_g"], w["ik_g"], min(512, n))

    k3 = ka.reshape(b, t, H_KV * HD_A)
    v3 = va.reshape(b, t, H_KV * HD_A)
    ki3 = ki.reshape(b, t, D_IDX)
    past = 0
    if past_k is not None:
        past = past_k.shape[1]
        k3 = jnp.concatenate([past_k.reshape(b, past, -1), k3], axis=1)
        v3 = jnp.concatenate([past_v.reshape(b, past, -1), v3], axis=1)
        ki3 = jnp.concatenate([past_ki, ki3], axis=1)
    s_valid = past + t
    tk = 512
    pad = ((0, 0), (0, _round_up(s_valid, tk) - s_valid), (0, 0))
    k_all = jnp.pad(k3.astype(BF16), pad).reshape(b, -1, H_KV, HD_A).transpose(0, 2, 3, 1)
    ones = jnp.ones((b, s_valid, HD_A), BF16)
    v_parts = []
    for g in range(H_KV):
        v_parts += [v3[..., g * HD_A:(g + 1) * HD_A].astype(BF16), ones]
    v_all = jnp.pad(jnp.concatenate(v_parts, axis=-1), pad)
    ki2 = jnp.pad(jnp.concatenate([ki3, ki3], axis=-1).astype(BF16), pad).transpose(0, 2, 1)
    oa = _dsa(p, k_all, v_all, ki2, w["q_g"], b=b, t=t, tq=min(128, t), tk=tk,
              s_valid=s_valid, past=past, top_k=min(TOPK_MAX, s_valid // 4))

    chunk = CHUNK if t % CHUNK == 0 else t
    tt = min(512, t)
    og, s_new = _gla(p, w["w_up"], w["b_up"], w["gla_g"], s0, b=b, t=t, tt=tt, chunk=chunk)
    om = _mem_attn(p, mk, mv, w["mq_g"], b=b, t=t, tt=tt)
    x1, h2, cb = _merge(x2, oa, og, om, p, w["w_a"], w["w_g"], w["w_m"], w["w_o"],
                        w["ffn_g"], w["w_r"], w["b_r"], min(256, n))
    y = _moe(h2, cb, x1, w["w_eg"], w["w_eu"], w["w_ed"], min(1024, n))
    return (y.reshape(b, t, d), ka.reshape(b, t, H_KV, HD_A), va.reshape(b, t, H_KV, HD_A),
            ki.reshape(b, t, D_IDX), s_new)


def _prep_weights(l, attn_norm_g, w_in, q_norm_g, k_norm_g, idx_k_norm_g, w_gla_gate_up,
                  b_gla_gate_up, gla_out_norm_g, mem_q_norm_g, w_branch_a, w_branch_g,
                  w_branch_m, w_out, ffn_norm_g, w_group_router, b_group_router,
                  w_expert_router, b_expert_router, w_exp_gate, w_exp_up, w_exp_down):
    d = D_MODEL
    wi = w_in[l]
    cols = [wi[:, _SRC[name][0]:_SRC[name][1]] for name in _DST_ORDER]
    used = sum(c.shape[1] for c in cols)
    w_cat = jnp.concatenate(cols + [jnp.zeros((d, C_PAD - used), F32)], axis=1).astype(BF16)
    pad_r = LANES - N_EXPERTS - N_GROUPS
    w_r = jnp.concatenate([w_expert_router[l], w_group_router[l], jnp.zeros((d, pad_r), F32)], axis=1)
    b_r = jnp.concatenate([b_expert_router[l], b_group_router[l], jnp.zeros((pad_r,), F32)])
    return dict(
        attn_g=attn_norm_g[l][None], w_cat=w_cat, q_g=q_norm_g[l][None], k_g=k_norm_g[l][None],
        ik_g=idx_k_norm_g[l][None],
        w_up=w_gla_gate_up[l].astype(BF16), b_up=b_gla_gate_up[l][None],
        gla_g=gla_out_norm_g[l][None],
        mq_g=mem_q_norm_g[l][None],
        w_a=w_branch_a[l].astype(BF16), w_g=w_branch_g[l].astype(BF16),
        w_m=w_branch_m[l].astype(BF16), w_o=w_out[l].astype(BF16),
        ffn_g=ffn_norm_g[l][None], w_r=w_r.astype(BF16), b_r=b_r[None],
        w_eg=w_exp_gate[l].astype(BF16), w_eu=w_exp_up[l].astype(BF16),
        w_ed=w_exp_down[l].astype(BF16))


def kernel(x_prompt, x_sample, mem_prompt, cache_k, cache_v, cache_idx_k, state_gla, cache_mem_k, cache_mem_v, attn_norm_g, w_in, q_norm_g, k_norm_g, idx_k_norm_g, w_gla_gate_up, b_gla_gate_up, gla_out_norm_g, mem_norm_g, w_mem_kv, mem_q_norm_g, mem_k_norm_g, w_branch_a, w_branch_g, w_branch_m, w_out, ffn_norm_g, w_group_router, b_group_router, w_expert_router, b_expert_router, w_exp_gate, w_exp_up, w_exp_down):
    depth = w_in.shape[0]
    y_p, y_s = x_prompt, x_sample
    outs = [[] for _ in range(10)]
    for l in range(depth):
        w = _prep_weights(l, attn_norm_g, w_in, q_norm_g, k_norm_g, idx_k_norm_g, w_gla_gate_up,
                          b_gla_gate_up, gla_out_norm_g, mem_q_norm_g, w_branch_a, w_branch_g,
                          w_branch_m, w_out, ffn_norm_g, w_group_router, b_group_router,
                          w_expert_router, b_expert_router, w_exp_gate, w_exp_up, w_exp_down)
        bp, n_mem, d = mem_prompt.shape
        mk_p, mv_p = _mem_kv(mem_prompt.reshape(bp * n_mem, d), mem_norm_g[l][None],
                             w_mem_kv[l].astype(BF16), mem_k_norm_g[l][None], min(256, bp * n_mem))
        mk_p = mk_p.reshape(bp, n_mem, H_M, HD_M)
        mv_p = mv_p.reshape(bp, n_mem, H_M, HD_M)
        s0 = jnp.zeros((bp, H_G, DK_G, DV_G), F32)
        y_p, k_new, v_new, ki_new, s_new = _layer(
            y_p, mk_p.reshape(bp, n_mem, -1).astype(BF16), mv_p.reshape(bp, n_mem, -1).astype(BF16),
            None, None, None, s0, w)
        for lst, val in zip(outs[:6], (k_new, v_new, ki_new, s_new, mk_p, mv_p)):
            lst.append(val)
        bs = x_sample.shape[0]
        y_s, k_new, v_new, ki_new, s_new = _layer(
            y_s, cache_mem_k[l].reshape(bs, n_mem, -1).astype(BF16),
            cache_mem_v[l].reshape(bs, n_mem, -1).astype(BF16),
            cache_k[l], cache_v[l], cache_idx_k[l], state_gla[l], w)
        for lst, val in zip(outs[6:], (k_new, v_new, ki_new, s_new)):
            lst.append(val)
    return (y_p, y_s) + tuple(jnp.stack(o) for o in outs)
```

```python
import functools

import numpy as np
import jax
import jax.numpy as jnp
from jax import lax
from jax.experimental import pallas as pl
from jax.experimental.pallas import tpu as pltpu

F32 = jnp.float32
BF16 = jnp.bfloat16

D_MODEL = 1024
CHUNK = 64
CHUNK_SHIFT = 6
TOPK_MAX = 256
H_A, HD_A, H_KV = 8, 128, 2
H_IDX, D_IDX = 16, 64
H_G, DK_G, DV_G = 4, 128, 256
GATE_RANK, GATE_TEMP = 16, 16.0
H_M, HD_M = 4, 256
N_GROUPS, E_PER_GROUP = 4, 8
N_EXPERTS = N_GROUPS * E_PER_GROUP
D_EXPERT = 256
EPS = 1e-6
NEG_INF = -1e30
LANES = 128
LOG2E = 1.4426950408889634
GROUP_LANE = LANES - 1
COUNT_ROWS = 128
DSA_TQ = 128
MOE_SUB = 128
MOE_E_STEP = 4

QA_OFF, QI_OFF, QG_OFF, KG_OFF, VG_OFF = 0, 1024, 2048, 2560, 3072
GO_OFF, QM_OFF, GB_OFF, KA_OFF, VA_OFF, TAIL_OFF = 4096, 5120, 6144, 9216, 9472, 9728
TAIL_KI, TAIL_WI, TAIL_GLR = 0, 64, 80
C_PAD = 10240
_SRC = dict(qa=(0, 1024), ka=(1024, 1280), va=(1280, 1536), qi=(1536, 2560), ki=(2560, 2624),
            wi=(2624, 2640), qg=(2640, 3152), kg=(3152, 3664), vg=(3664, 4688), glr=(4688, 4704),
            go=(4704, 5728), qm=(5728, 6752), gb=(6752, 9824))
_DST_ORDER = ("qa", "qi", "qg", "kg", "vg", "go", "qm", "gb", "ka", "va", "ki", "wi", "glr")

INT_MIN = -2 ** 31
_KEY_VALID = int(np.float32(NEG_INF * 0.5).view(np.int32)) ^ 0x7FFFFFFF
if _KEY_VALID >= 2 ** 31:
    _KEY_VALID -= 2 ** 32

VMEM_LIMIT = 56 * 1024 * 1024


def _params(*sem):
    return pltpu.CompilerParams(dimension_semantics=sem, vmem_limit_bytes=VMEM_LIMIT)


def _rms(x, g):
    return x * lax.rsqrt(jnp.mean(x * x, axis=-1, keepdims=True) + EPS) * g


def _dot(a, b):
    return jnp.dot(a, b, preferred_element_type=F32)


def _dot_nt(a, b):
    return lax.dot_general(a, b, (((1,), (1,)), ((), ())), preferred_element_type=F32)


def _dot_tn(a, b):
    return lax.dot_general(a, b, (((0,), (0,)), ((), ())), preferred_element_type=F32)


def _norm_proj_kernel(x_ref, g_ref, w_ref, o_ref, h_ref):
    @pl.when(pl.program_id(1) == 0)
    def _():
        h_ref[...] = _rms(x_ref[...], g_ref[...]).astype(BF16)

    o_ref[...] = _dot(h_ref[...], w_ref[...])


def _norm_proj(x, g, w, tm, tn):
    n, d = x.shape
    c = w.shape[1]
    return pl.pallas_call(
        _norm_proj_kernel,
        grid=(n // tm, c // tn),
        in_specs=[pl.BlockSpec((tm, d), lambda i, j: (i, 0)),
                  pl.BlockSpec((1, d), lambda i, j: (0, 0)),
                  pl.BlockSpec((d, tn), lambda i, j: (0, j))],
        out_specs=pl.BlockSpec((tm, tn), lambda i, j: (i, j)),
        out_shape=jax.ShapeDtypeStruct((n, c), F32),
        scratch_shapes=[pltpu.VMEM((tm, d), BF16)],
        compiler_params=_params("parallel", "arbitrary"),
        name="norm_proj",
    )(x, g, w)


def _kv_post_kernel(kv_ref, tail_ref, kg_ref, ig_ref, k_ref, v_ref, ki_ref):
    kv = kv_ref[...]
    for h in range(H_KV):
        k_ref[:, h * HD_A:(h + 1) * HD_A] = _rms(kv[:, h * HD_A:(h + 1) * HD_A], kg_ref[...])
    v_ref[...] = kv[:, H_KV * HD_A:]
    ki_ref[...] = _rms(tail_ref[:, TAIL_KI:TAIL_KI + D_IDX], ig_ref[...])


def _kv_post(p, k_g, ik_g, tm):
    n = p.shape[0]
    w = 2 * H_KV * HD_A
    return pl.pallas_call(
        _kv_post_kernel,
        grid=(n // tm,),
        in_specs=[pl.BlockSpec((tm, w), lambda i: (i, KA_OFF // w)),
                  pl.BlockSpec((tm, LANES), lambda i: (i, TAIL_OFF // LANES)),
                  pl.BlockSpec((1, HD_A), lambda i: (0, 0)),
                  pl.BlockSpec((1, D_IDX), lambda i: (0, 0))],
        out_specs=[pl.BlockSpec((tm, H_KV * HD_A), lambda i: (i, 0)),
                   pl.BlockSpec((tm, H_KV * HD_A), lambda i: (i, 0)),
                   pl.BlockSpec((tm, D_IDX), lambda i: (i, 0))],
        out_shape=[jax.ShapeDtypeStruct((n, H_KV * HD_A), F32),
                   jax.ShapeDtypeStruct((n, H_KV * HD_A), F32),
                   jax.ShapeDtypeStruct((n, D_IDX), F32)],
        compiler_params=_params("parallel"),
        name="kv_post",
    )(p, p, k_g, ik_g)


def _to_key(x):
    x = jnp.where(x == 0.0, 0.0, x)
    b = lax.bitcast_convert_type(x, jnp.int32)
    return jnp.where(b < 0, b ^ 0x7FFFFFFF, b)


def _dsa_kernel(qa_ref, qi_ref, tail_ref, k_ref, v_ref, ki_ref, qg_ref, o_ref,
                key_ref, m_ref, acc_ref, *, tq, tk, s_valid, past, top_k):
    i = pl.program_id(1)
    q0 = past + i * tq
    lim = jnp.minimum((((q0 + tq - 1) >> CHUNK_SHIFT) + 1) << CHUNK_SHIFT, s_valid)
    n_tiles = (lim + tk - 1) // tk
    lane = lax.broadcasted_iota(jnp.int32, (tq, LANES), 1)
    q_chunk = (q0 + lax.broadcasted_iota(jnp.int32, (tq, 1), 0)) >> CHUNK_SHIFT

    qi = qi_ref[...]
    qi_h = []
    for j in range(H_IDX // 2):
        pair = qi[:, j * LANES:(j + 1) * LANES] * (D_IDX ** -0.5)
        qi_h.append(jnp.where(lane < D_IDX, pair, 0.0).astype(BF16))
        qi_h.append(jnp.where(lane >= D_IDX, pair, 0.0).astype(BF16))
    wi = tail_ref[:, TAIL_WI:TAIL_WI + H_IDX] * (H_IDX ** -0.5)

    def tile_off(j):
        return pl.multiple_of(j * tk, tk)

    def score_tile(j, carry):
        off = tile_off(j)
        kt = ki_ref[:, pl.ds(off, tk)]
        acc = jnp.zeros((tq, tk), F32)
        for h in range(H_IDX):
            acc = acc + jnp.maximum(_dot(qi_h[h], kt), 0.0) * wi[:, h:h + 1]
        kpos = off + lax.broadcasted_iota(jnp.int32, (1, tk), 1)
        adm = ((kpos >> CHUNK_SHIFT) <= q_chunk) & (kpos < s_valid)
        key_ref[:, pl.ds(off, tk)] = _to_key(jnp.where(adm, acc, NEG_INF))
        return carry

    lax.fori_loop(0, n_tiles, score_tile, 0)

    rb = min(tq, COUNT_ROWS)

    def count(cmp, thr):
        outs = []
        for r0 in range(0, tq, rb):
            th = thr[r0:r0 + rb]

            def body(j, c, r0=r0, th=th):
                m = jnp.where(cmp(key_ref[r0:r0 + rb, pl.ds(tile_off(j), tk)], th), 1.0, 0.0)
                part = m[:, 0:LANES]
                for u in range(1, tk // LANES):
                    part = part + m[:, u * LANES:(u + 1) * LANES]
                return c + part

            c = lax.fori_loop(0, n_tiles, body, jnp.zeros((rb, LANES), F32))
            outs.append(jnp.sum(c, axis=-1, keepdims=True))
        return outs[0] if len(outs) == 1 else jnp.concatenate(outs, axis=0)

    ge = lambda a, b: a >= b
    gt = lambda a, b: a > b

    def bit_step(it, v):
        trial = v ^ lax.shift_left(jnp.int32(1), 31 - it)
        return jnp.where(count(ge, trial) >= float(top_k), trial, v)

    v = lax.fori_loop(0, 32, bit_step, jnp.full((tq, 1), INT_MIN, jnp.int32))

    excess = (count(ge, v) > float(top_k)) & (v > _KEY_VALID)

    @pl.when(jnp.max(jnp.where(excess, 1.0, 0.0)) > 0.0)
    def _():
        need = float(top_k) - count(gt, v)
        r_i = lax.broadcasted_iota(jnp.int32, (tk, tk), 0)
        c_i = lax.broadcasted_iota(jnp.int32, (tk, tk), 1)
        before = jnp.where(r_i < c_i, 1.0, 0.0).astype(BF16)

        def body(j, seen):
            off = tile_off(j)
            kt = key_ref[:, pl.ds(off, tk)]
            eq = kt == v
            eqf = jnp.where(eq, 1.0, 0.0)
            rank = seen + _dot(eqf.astype(BF16), before)
            key_ref[:, pl.ds(off, tk)] = jnp.where(eq & (rank >= need), v - 1, kt)
            return seen + jnp.sum(eqf, axis=-1, keepdims=True)

        lax.fori_loop(0, n_tiles, body, jnp.zeros((tq, 1), F32))

    qa = qa_ref[...]
    rep = H_A // H_KV
    qs = []
    for g in range(H_KV):
        heads = [(_rms(qa[:, h * HD_A:(h + 1) * HD_A], qg_ref[...]) * (HD_A ** -0.5 * LOG2E)).astype(BF16)
                 for h in range(g * rep, (g + 1) * rep)]
        qs.append(jnp.concatenate(heads, axis=0))
    m_ref[...] = jnp.full(m_ref.shape, NEG_INF, F32)
    acc_ref[...] = jnp.zeros(acc_ref.shape, F32)

    def att_tile(j, carry):
        off = tile_off(j)
        kt_key = key_ref[:, pl.ds(off, tk)]
        bias = jnp.where((kt_key >= v) & (kt_key > _KEY_VALID), 0.0, NEG_INF)
        logits = [_dot(qs[g], k_ref[g, :, pl.ds(off, tk)]) for g in range(H_KV)]
        for g in range(H_KV):
            vt = v_ref[pl.ds(off, tk), 2 * g * HD_A:2 * (g + 1) * HD_A]
            s = (logits[g].reshape(rep, tq, tk) + bias[None]).reshape(rep * tq, tk)
            m_old = m_ref[g]
            m_new = jnp.maximum(m_old, jnp.max(s, axis=-1, keepdims=True))
            m_sub = jnp.where(m_new > NEG_INF * 0.5, m_new, -NEG_INF)
            p = jnp.concatenate([jnp.exp2(s[:, u * LANES:(u + 1) * LANES] - m_sub)
                                 for u in range(tk // LANES)], axis=1)
            alpha = jnp.exp2(m_old - m_new)
            pv = _dot(p.astype(BF16), vt)
            for u in range(2):
                cols = slice(u * HD_A, (u + 1) * HD_A)
                acc_ref[g, :, cols] = alpha * acc_ref[g, :, cols] + pv[:, cols]
            m_ref[g] = m_new
        return carry

    lax.fori_loop(0, n_tiles, att_tile, 0)
    for g in range(H_KV):
        out = acc_ref[g, :, 0:HD_A] / acc_ref[g, :, HD_A:2 * HD_A]
        for r in range(rep):
            h = g * rep + r
            o_ref[:, h * HD_A:(h + 1) * HD_A] = out[r * tq:(r + 1) * tq].astype(o_ref.dtype)


def _dsa(p, k_all, v_all, ki2, q_g, *, b, t, tq, tk, s_valid, past, top_k):
    nq = t // tq
    s_pad = k_all.shape[-1]
    wq = H_A * HD_A
    kern = functools.partial(_dsa_kernel, tq=tq, tk=tk, s_valid=s_valid, past=past, top_k=top_k)
    return pl.pallas_call(
        kern,
        grid=(b, nq),
        in_specs=[pl.BlockSpec((tq, wq), lambda bb, i: (bb * nq + i, QA_OFF // wq)),
                  pl.BlockSpec((tq, wq), lambda bb, i: (bb * nq + i, QI_OFF // wq)),
                  pl.BlockSpec((tq, LANES), lambda bb, i: (bb * nq + i, TAIL_OFF // LANES)),
                  pl.BlockSpec((None, H_KV, HD_A, s_pad), lambda bb, i: (bb, 0, 0, 0)),
                  pl.BlockSpec((None, s_pad, 2 * H_KV * HD_A), lambda bb, i: (bb, 0, 0)),
                  pl.BlockSpec((None, LANES, s_pad), lambda bb, i: (bb, 0, 0)),
                  pl.BlockSpec((1, HD_A), lambda bb, i: (0, 0))],
        out_specs=pl.BlockSpec((tq, wq), lambda bb, i: (bb * nq + i, 0)),
        out_shape=jax.ShapeDtypeStruct((b * t, wq), BF16),
        scratch_shapes=[pltpu.VMEM((tq, s_pad), jnp.int32),
                        pltpu.VMEM((H_KV, H_A // H_KV * tq, LANES), F32),
                        pltpu.VMEM((H_KV, H_A // H_KV * tq, 2 * HD_A), F32)],
        compiler_params=_params("parallel", "arbitrary"),
        name="dsa",
    )(p, p, p, k_all, v_all, ki2, q_g)


def _fold8(x, op):
    parts = [x[r:r + 8] for r in range(0, x.shape[0], 8)]
    while len(parts) > 1:
        nxt = [op(parts[a], parts[a + 1]) for a in range(0, len(parts) - 1, 2)]
        if len(parts) % 2:
            nxt.append(parts[-1])
        parts = nxt
    return parts[0]


def _rowsum8(x):
    return _fold8(x, jnp.add)


def _rowmax8(x):
    return _fold8(x, jnp.maximum)


def _dsa_t_kernel(qa_ref, qi_ref, tail_ref, k_ref, vt_ref, ki_ref, qg_ref, o_ref,
                  key_ref, m_ref, l_ref, acc_ref, *, tq, tk, s_valid, past, top_k):
    i = pl.program_id(1)
    q0 = past + i * tq
    lim = jnp.minimum((((q0 + tq - 1) >> CHUNK_SHIFT) + 1) << CHUNK_SHIFT, s_valid)
    n_tiles = (lim + tk - 1) // tk
    q_chunk = (q0 + lax.broadcasted_iota(jnp.int32, (1, tq), 1)) >> CHUNK_SHIFT
    sub = lax.broadcasted_iota(jnp.int32, (LANES, tq), 0)

    qi = qi_ref[...]
    qi_t = []
    for j in range(H_IDX // 2):
        pair_t = (qi[:, j * LANES:(j + 1) * LANES] * (D_IDX ** -0.5)).T
        qi_t.append(jnp.concatenate([jnp.where(sub < D_IDX, pair_t, 0.0),
                                     jnp.where(sub >= D_IDX, pair_t, 0.0)], axis=1).astype(BF16))
    wi_t = tail_ref[...].T * (H_IDX ** -0.5)

    def tile_off(j):
        return pl.multiple_of(j * tk, tk)

    def score_tile(j, carry):
        off = tile_off(j)
        kt = ki_ref[pl.ds(off, tk), :]
        acc = jnp.zeros((tk, tq), F32)
        for jj in range(H_IDX // 2):
            r = jnp.maximum(_dot(kt, qi_t[jj]), 0.0)
            h = TAIL_WI + 2 * jj
            acc = acc + r[:, 0:tq] * wi_t[h:h + 1, :] + r[:, tq:2 * tq] * wi_t[h + 1:h + 2, :]
        kpos = off + lax.broadcasted_iota(jnp.int32, (tk, 1), 0)
        adm = ((kpos >> CHUNK_SHIFT) <= q_chunk) & (kpos < s_valid)
        key_ref[pl.ds(off, tk), :] = _to_key(jnp.where(adm, acc, NEG_INF))
        return carry

    lax.fori_loop(0, n_tiles, score_tile, 0)

    def count(cmp, thr):
        def body(j, c):
            m = jnp.where(cmp(key_ref[pl.ds(tile_off(j), tk), :], thr), 1.0, 0.0)
            return c + _rowsum8(m)
        c = lax.fori_loop(0, n_tiles, body, jnp.zeros((8, tq), F32))
        return jnp.sum(c, axis=0, keepdims=True)

    ge = lambda a, b: a >= b
    gt = lambda a, b: a > b

    def bit_step(it, v):
        trial = v ^ lax.shift_left(jnp.int32(1), 31 - it)
        return jnp.where(count(ge, trial) >= float(top_k), trial, v)

    v = lax.fori_loop(0, 32, bit_step, jnp.full((1, tq), INT_MIN, jnp.int32))

    excess = (count(ge, v) > float(top_k)) & (v > _KEY_VALID)

    @pl.when(jnp.max(jnp.where(excess, 1.0, 0.0)) > 0.0)
    def _():
        need = float(top_k) - count(gt, v)
        r_i = lax.broadcasted_iota(jnp.int32, (tk, tk), 0)
        c_i = lax.broadcasted_iota(jnp.int32, (tk, tk), 1)
        before = jnp.where(c_i < r_i, 1.0, 0.0).astype(BF16)

        def body(j, seen):
            rows = pl.ds(tile_off(j), tk)
            kt = key_ref[rows, :]
            eq = kt == v
            eqf = jnp.where(eq, 1.0, 0.0)
            rank = seen + _dot(before, eqf.astype(BF16))
            key_ref[rows, :] = jnp.where(eq & (rank >= need), v - 1, kt)
            return seen + jnp.sum(_rowsum8(eqf), axis=0, keepdims=True)

        lax.fori_loop(0, n_tiles, body, jnp.zeros((1, tq), F32))

    rep = H_A // H_KV
    qa = qa_ref[...]
    q_t = []
    for g in range(H_KV):
        heads = [(_rms(qa[:, h * HD_A:(h + 1) * HD_A], qg_ref[...]) * (HD_A ** -0.5 * LOG2E)).T
                 for h in range(g * rep, (g + 1) * rep)]
        q_t.append(jnp.concatenate(heads, axis=1).astype(BF16))
    m_ref[...] = jnp.full(m_ref.shape, NEG_INF, F32)
    l_ref[...] = jnp.zeros(l_ref.shape, F32)
    acc_ref[...] = jnp.zeros(acc_ref.shape, F32)

    def att_tile(j, carry):
        off = tile_off(j)
        rows = pl.ds(off, tk)
        kt_key = key_ref[rows, :]
        bias = jnp.where((kt_key >= v) & (kt_key > _KEY_VALID), 0.0, NEG_INF)
        logits = [_dot(k_ref[rows, g * HD_A:(g + 1) * HD_A], q_t[g]) for g in range(H_KV)]
        for g in range(H_KV):
            m_old = m_ref[g]
            ps, ms, ls = [], [], []
            for r in range(rep):
                cols = slice(r * tq, (r + 1) * tq)
                s = logits[g][:, cols] + bias
                m_o = m_old[:, cols]
                m_n = jnp.maximum(m_o, jnp.max(_rowmax8(s), axis=0, keepdims=True))
                p = jnp.exp2(s - jnp.where(m_n > NEG_INF * 0.5, m_n, -NEG_INF)[0:1])
                ps.append(p.astype(BF16))
                ms.append(m_n)
                ls.append(jnp.sum(_rowsum8(p), axis=0, keepdims=True))
            m_new = jnp.concatenate(ms, axis=1)
            alpha = jnp.exp2(m_old - m_new)
            l_ref[g] = alpha * l_ref[g] + jnp.concatenate(ls, axis=1)
            pv = _dot(vt_ref[g, :, rows], jnp.concatenate(ps, axis=1))
            acc_ref[g] = alpha[0:1] * acc_ref[g] + pv
            m_ref[g] = m_new
        return carry

    lax.fori_loop(0, n_tiles, att_tile, 0)
    for g in range(H_KV):
        out = acc_ref[g] / l_ref[g][0:1]
        for r in range(rep):
            h = g * rep + r
            o_ref[:, h * HD_A:(h + 1) * HD_A] = out[:, r * tq:(r + 1) * tq].T.astype(o_ref.dtype)


def _dsa_t(p, k_all, vt_all, ki2, q_g, *, b, t, tq, tk, s_valid, past, top_k):
    nq = t // tq
    s_pad = k_all.shape[1]
    wq = H_A * HD_A
    rep = H_A // H_KV
    kern = functools.partial(_dsa_t_kernel, tq=tq, tk=tk, s_valid=s_valid, past=past, top_k=top_k)
    return pl.pallas_call(
        kern,
        grid=(b, nq),
        in_specs=[pl.BlockSpec((tq, wq), lambda bb, i: (bb * nq + i, QA_OFF // wq)),
                  pl.BlockSpec((tq, wq), lambda bb, i: (bb * nq + i, QI_OFF // wq)),
                  pl.BlockSpec((tq, LANES), lambda bb, i: (bb * nq + i, TAIL_OFF // LANES)),
                  pl.BlockSpec((None, s_pad, H_KV * HD_A), lambda bb, i: (bb, 0, 0)),
                  pl.BlockSpec((None, H_KV, HD_A, s_pad), lambda bb, i: (bb, 0, 0, 0)),
                  pl.BlockSpec((None, s_pad, LANES), lambda bb, i: (bb, 0, 0)),
                  pl.BlockSpec((1, HD_A), lambda bb, i: (0, 0))],
        out_specs=pl.BlockSpec((tq, wq), lambda bb, i: (bb * nq + i, 0)),
        out_shape=jax.ShapeDtypeStruct((b * t, wq), BF16),
        scratch_shapes=[pltpu.VMEM((s_pad, tq), jnp.int32),
                        pltpu.VMEM((H_KV, 8, rep * tq), F32),
                        pltpu.VMEM((H_KV, 8, rep * tq), F32),
                        pltpu.VMEM((H_KV, HD_A, rep * tq), F32)],
        compiler_params=_params("parallel", "arbitrary"),
        name="dsa_t",
    )(p, p, p, k_all, vt_all, ki2, q_g)


def _gla_kernel(q_ref, k_ref, v_ref, go_ref, tail_ref, wup_ref, bup_ref, ng_ref, s0_ref,
                o_ref, s_ref, la_ref, *, tt, chunk):
    @pl.when(pl.program_id(1) == 0)
    def _():
        s_ref[...] = s0_ref[...]

    glr = tail_ref[:, TAIL_GLR:TAIL_GLR + GATE_RANK].astype(BF16)
    z = _dot(glr, wup_ref[...]) + bup_ref[...]
    la_ref[...] = (jnp.minimum(z, 0.0) - jnp.log1p(jnp.exp(-jnp.abs(z)))) / GATE_TEMP

    width = H_G * DK_G
    row = lax.broadcasted_iota(jnp.int32, (chunk, width), 0)
    causal = (lax.broadcasted_iota(jnp.int32, (chunk, chunk), 0)
              >= lax.broadcasted_iota(jnp.int32, (chunk, chunk), 1))
    eye = (lax.broadcasted_iota(jnp.int32, (DK_G, DK_G), 0)
           == lax.broadcasted_iota(jnp.int32, (DK_G, DK_G), 1))

    def body(c, carry):
        off = pl.multiple_of(c * chunk, chunk)
        rows = pl.ds(off, chunk)
        bcum = la_ref[rows, :]
        step = 1
        while step < chunk:
            bcum = bcum + jnp.where(row >= step, pltpu.roll(bcum, step, axis=0), 0.0)
            step *= 2
        b_last = bcum[chunk - 1:chunk, :]
        q = q_ref[rows, :] * (DK_G ** -0.5)
        k = k_ref[rows, :]
        qe = (q * jnp.exp(bcum)).astype(BF16)
        ke = (k * jnp.exp(-bcum)).astype(BF16)
        kd = (k * jnp.exp(b_last - bcum)).astype(BF16)
        decay = jnp.exp(b_last)
        hk = [slice(h * DK_G, (h + 1) * DK_G) for h in range(H_G)]
        hv = [slice(h * DV_G, (h + 1) * DV_G) for h in range(H_G)]
        att = [jnp.where(causal, _dot_nt(qe[:, hk[h]], ke[:, hk[h]]), 0.0).astype(BF16)
               for h in range(H_G)]
        vv = [v_ref[rows, hv[h]].astype(BF16) for h in range(H_G)]
        state = [s_ref[h] for h in range(H_G)]
        inter = [_dot(qe[:, hk[h]], state[h].astype(BF16)) for h in range(H_G)]
        intra = [_dot(att[h], vv[h]) for h in range(H_G)]
        upd = [_dot_tn(kd[:, hk[h]], vv[h]) for h in range(H_G)]
        for h in range(H_G):
            decay_col = jnp.sum(jnp.where(eye, jnp.broadcast_to(decay[:, hk[h]], (DK_G, DK_G)), 0.0),
                                axis=1, keepdims=True)
            s_ref[h] = state[h] * decay_col + upd[h]
            go = go_ref[rows, hv[h]]
            o_ref[rows, hv[h]] = (_rms(intra[h] + inter[h], ng_ref[...])
                                  * (go * jax.nn.sigmoid(go))).astype(o_ref.dtype)
        return carry

    lax.fori_loop(0, tt // chunk, body, 0)


def _gla(p, w_up, b_up, n_g, s0, *, b, t, tt, chunk):
    nt = t // tt
    kern = functools.partial(_gla_kernel, tt=tt, chunk=chunk)
    wk, wv = H_G * DK_G, H_G * DV_G
    rowblk = lambda w, off: pl.BlockSpec((tt, w), lambda bb, i: (bb * nt + i, off // w))
    state = pl.BlockSpec((None, H_G, DK_G, DV_G), lambda bb, i: (bb, 0, 0, 0))
    return pl.pallas_call(
        kern,
        grid=(b, nt),
        in_specs=[rowblk(wk, QG_OFF), rowblk(wk, KG_OFF), rowblk(wv, VG_OFF), rowblk(wv, GO_OFF),
                  rowblk(LANES, TAIL_OFF),
                  pl.BlockSpec((GATE_RANK, wk), lambda bb, i: (0, 0)),
                  pl.BlockSpec((1, wk), lambda bb, i: (0, 0)),
                  pl.BlockSpec((1, DV_G), lambda bb, i: (0, 0)),
                  state],
        out_specs=[pl.BlockSpec((tt, wv), lambda bb, i: (bb * nt + i, 0)), state],
        out_shape=[jax.ShapeDtypeStruct((b * t, wv), BF16),
                   jax.ShapeDtypeStruct((b, H_G, DK_G, DV_G), F32)],
        scratch_shapes=[pltpu.VMEM((tt, wk), F32)],
        compiler_params=_params("parallel", "arbitrary"),
        name="gla",
    )(p, p, p, p, p, w_up, b_up, n_g, s0)


def _mem_kv_kernel(x_ref, g_ref, w_ref, kg_ref, mk_ref, mv_ref):
    kv = _dot(_rms(x_ref[...], g_ref[...]).astype(BF16), w_ref[...])
    for h in range(H_M):
        mk_ref[:, h * HD_M:(h + 1) * HD_M] = _rms(kv[:, h * HD_M:(h + 1) * HD_M], kg_ref[...])
    mv_ref[...] = kv[:, H_M * HD_M:]


def _mem_kv(mem, g, w, k_g, tm):
    n, d = mem.shape
    wk = H_M * HD_M
    return pl.pallas_call(
        _mem_kv_kernel,
        grid=(n // tm,),
        in_specs=[pl.BlockSpec((tm, d), lambda i: (i, 0)),
                  pl.BlockSpec((1, d), lambda i: (0, 0)),
                  pl.BlockSpec((d, 2 * wk), lambda i: (0, 0)),
                  pl.BlockSpec((1, HD_M), lambda i: (0, 0))],
        out_specs=[pl.BlockSpec((tm, wk), lambda i: (i, 0)),
                   pl.BlockSpec((tm, wk), lambda i: (i, 0))],
        out_shape=[jax.ShapeDtypeStruct((n, wk), F32), jax.ShapeDtypeStruct((n, wk), F32)],
        compiler_params=_params("parallel"),
        name="mem_kv",
    )(mem, g, w, k_g)


def _mem_attn_kernel(q_ref, mk_ref, mv_ref, g_ref, o_ref):
    qm = q_ref[...]
    for h in range(H_M):
        cols = slice(h * HD_M, (h + 1) * HD_M)
        q = (_rms(qm[:, cols], g_ref[...]) * (HD_M ** -0.5)).astype(BF16)
        s = _dot_nt(q, mk_ref[:, cols])
        e = jnp.exp(s - jnp.max(s, axis=-1, keepdims=True))
        prob = e / jnp.sum(e, axis=-1, keepdims=True)
        o_ref[:, cols] = _dot(prob.astype(BF16), mv_ref[:, cols]).astype(o_ref.dtype)


def _mem_attn(p, mk, mv, q_g, *, b, t, tt):
    nt = t // tt
    wq = H_M * HD_M
    n_mem = mk.shape[1]
    return pl.pallas_call(
        _mem_attn_kernel,
        grid=(b, nt),
        in_specs=[pl.BlockSpec((tt, wq), lambda bb, i: (bb * nt + i, QM_OFF // wq)),
                  pl.BlockSpec((None, n_mem, wq), lambda bb, i: (bb, 0, 0)),
                  pl.BlockSpec((None, n_mem, wq), lambda bb, i: (bb, 0, 0)),
                  pl.BlockSpec((1, HD_M), lambda bb, i: (0, 0))],
        out_specs=pl.BlockSpec((tt, wq), lambda bb, i: (bb * nt + i, 0)),
        out_shape=jax.ShapeDtypeStruct((b * t, wq), BF16),
        compiler_params=_params("parallel", "arbitrary"),
        name="mem_attn",
    )(p, mk, mv, q_g)


def _merge_kernel(x_ref, oa_ref, og_ref, om_ref, gb_ref, wa_ref, wg_ref, wm_ref, wo_ref,
                  fg_ref, wr_ref, br_ref, x1_ref, h2_ref, cb_ref):
    d = D_MODEL
    merged = (jax.nn.sigmoid(gb_ref[:, 0:d]) * _dot(oa_ref[...], wa_ref[...])
              + jax.nn.sigmoid(gb_ref[:, d:2 * d]) * _dot(og_ref[...], wg_ref[...])
              + jax.nn.sigmoid(gb_ref[:, 2 * d:3 * d]) * _dot(om_ref[...], wm_ref[...]))
    x1 = x_ref[...] + _dot(merged.astype(BF16), wo_ref[...])
    x1_ref[...] = x1
    hb = _rms(x1, fg_ref[...]).astype(BF16)
    h2_ref[...] = hb

    lg = _dot(hb, wr_ref[...]) + br_ref[...]
    lane = lax.broadcasted_iota(jnp.int32, lg.shape, 1).astype(F32)
    big = 1e9
    gmask = (lane >= N_EXPERTS) & (lane < N_EXPERTS + N_GROUPS)
    gl = jnp.where(gmask, lg, NEG_INF)
    gmax = jnp.max(gl, axis=-1, keepdims=True)
    gidx = jnp.min(jnp.where(gmask & (gl == gmax), lane, big), axis=-1, keepdims=True) - N_EXPERTS
    g_w = 1.0 / jnp.sum(jnp.where(gmask, jnp.exp(gl - gmax), 0.0), axis=-1, keepdims=True)
    lo = gidx * E_PER_GROUP
    emask = (lane >= lo) & (lane < lo + E_PER_GROUP)
    el = jnp.where(emask, lg, NEG_INF)
    ee = jnp.where(emask, jnp.exp(el - jnp.max(el, axis=-1, keepdims=True)), 0.0)
    ep = ee / jnp.sum(ee, axis=-1, keepdims=True)
    p1 = jnp.max(jnp.where(emask, ep, -1.0), axis=-1, keepdims=True)
    i1 = jnp.min(jnp.where(emask & (ep == p1), lane, big), axis=-1, keepdims=True)
    rest = emask & (lane != i1)
    p2 = jnp.max(jnp.where(rest, ep, -1.0), axis=-1, keepdims=True)
    i2 = jnp.min(jnp.where(rest & (ep == p2), lane, big), axis=-1, keepdims=True)
    den = p1 + p2
    cb_ref[...] = (jnp.where(lane == i1, g_w * (p1 / den), 0.0)
                   + jnp.where(lane == i2, g_w * (p2 / den), 0.0)
                   + jnp.where(lane == GROUP_LANE, gidx, 0.0))


def _merge(x, oa, og, om, p, wa, wg, wm, wo, f_g, w_r, b_r, tm):
    n, d = x.shape
    full = lambda shape: pl.BlockSpec(shape, lambda i: (0, 0))
    rowblk = lambda w: pl.BlockSpec((tm, w), lambda i: (i, 0))
    return pl.pallas_call(
        _merge_kernel,
        grid=(n // tm,),
        in_specs=[rowblk(d), rowblk(d), rowblk(d), rowblk(d),
                  pl.BlockSpec((tm, 3 * d), lambda i: (i, GB_OFF // (3 * d))),
                  full((d, d)), full((d, d)), full((d, d)), full((d, d)),
                  full((1, d)), full((d, LANES)), full((1, LANES))],
        out_specs=[rowblk(d), rowblk(d), rowblk(LANES)],
        out_shape=[jax.ShapeDtypeStruct((n, d), F32),
                   jax.ShapeDtypeStruct((n, d), BF16),
                   jax.ShapeDtypeStruct((n, LANES), F32)],
        compiler_params=_params("parallel"),
        name="merge",
    )(x, oa, og, om, p, wa, wg, wm, wo, f_g, w_r, b_r)


def _split3(x):
    hi = x.astype(BF16)
    r1 = x - hi.astype(F32)
    mid = r1.astype(BF16)
    lo = (r1 - mid.astype(F32)).astype(BF16)
    return hi, mid, lo


def _permute_rows(onehot, x):
    hi, mid, lo = _split3(x)
    return _dot(onehot, hi) + _dot(onehot, mid) + _dot(onehot, lo)


def _moe_kernel(h_ref, cb_ref, x1_ref, wg_ref, wu_ref, wd_ref, y_ref,
                hp_ref, cbp_ref, yp_ref, dest_ref, seg_ref):
    e = pl.program_id(1)
    tm = h_ref.shape[0]

    @pl.when(e == 0)
    def _():
        cb = cb_ref[...]
        g_row = cb.T[GROUP_LANE:GROUP_LANE + 1, :]
        grp = lax.broadcasted_iota(jnp.int32, (8, tm), 0).astype(F32)
        member = jnp.where(g_row == grp, 1.0, 0.0)
        r_i = lax.broadcasted_iota(jnp.int32, (tm, tm), 0)
        c_i = lax.broadcasted_iota(jnp.int32, (tm, tm), 1)
        upto = jnp.where(r_i <= c_i, 1.0, 0.0).astype(BF16)
        cnt = _dot(member.astype(BF16), upto)
        tot = cnt[:, tm - 1:tm]
        start = jnp.zeros((1, 1), F32)
        dest = jnp.zeros((1, tm), F32)
        seg_ref[0] = 0
        for j in range(N_GROUPS):
            dest = dest + member[j:j + 1] * (start + cnt[j:j + 1] - 1.0)
            start = start + tot[j:j + 1]
            seg_ref[j + 1] = jnp.sum(start).astype(jnp.int32)
        place = jnp.where(r_i.astype(F32) == dest, 1.0, 0.0).astype(BF16)
        hp_ref[...] = _dot(place, h_ref[...]).astype(BF16)
        cbp_ref[...] = _permute_rows(place, cb)
        dest_ref[...] = jnp.broadcast_to(dest, (LANES, tm)).T
        yp_ref[...] = jnp.zeros(yp_ref.shape, F32)

    sub = min(MOE_SUB, tm)
    n_e = wg_ref.shape[0]
    g = (e * n_e) // E_PER_GROUP
    first = seg_ref[g] // sub
    last = (seg_ref[g + 1] + sub - 1) // sub
    lane = lax.broadcasted_iota(jnp.int32, (sub, LANES), 1)

    def sub_tile(s, carry):
        rows = pl.ds(pl.multiple_of(s * sub, sub), sub)
        h = hp_ref[rows, :]
        cbp = cbp_ref[rows, :]
        gate = [_dot(h, wg_ref[k]) for k in range(n_e)]
        up = [_dot(h, wu_ref[k]) for k in range(n_e)]
        out = None
        for k in range(n_e):
            c = jnp.sum(jnp.where(lane == e * n_e + k, cbp, 0.0), axis=-1, keepdims=True)
            hid = (gate[k] * jax.nn.sigmoid(gate[k])) * up[k] * c
            part = _dot(hid.astype(BF16), wd_ref[k])
            out = part if out is None else out + part
        yp_ref[rows, :] += out
        return carry

    lax.fori_loop(first, last, sub_tile, 0)

    @pl.when(e == pl.num_programs(1) - 1)
    def _():
        col = lax.broadcasted_iota(jnp.int32, (tm, tm), 1).astype(F32)
        back = jnp.where(col == dest_ref[:, 0:1], 1.0, 0.0).astype(BF16)
        y_ref[...] = x1_ref[...] + _permute_rows(back, yp_ref[...])


def _moe(h2, cb, x1, wg, wu, wd, tm):
    n, d = x1.shape
    return pl.pallas_call(
        _moe_kernel,
        grid=(n // tm, N_EXPERTS // MOE_E_STEP),
        in_specs=[pl.BlockSpec((tm, d), lambda i, e: (i, 0)),
                  pl.BlockSpec((tm, LANES), lambda i, e: (i, 0)),
                  pl.BlockSpec((tm, d), lambda i, e: (i, 0)),
                  pl.BlockSpec((MOE_E_STEP, d, D_EXPERT), lambda i, e: (e, 0, 0)),
                  pl.BlockSpec((MOE_E_STEP, d, D_EXPERT), lambda i, e: (e, 0, 0)),
                  pl.BlockSpec((MOE_E_STEP, D_EXPERT, d), lambda i, e: (e, 0, 0))],
        out_specs=pl.BlockSpec((tm, d), lambda i, e: (i, 0)),
        out_shape=jax.ShapeDtypeStruct((n, d), F32),
        scratch_shapes=[pltpu.VMEM((tm, d), BF16),
                        pltpu.VMEM((tm, LANES), F32),
                        pltpu.VMEM((tm, d), F32),
                        pltpu.VMEM((tm, LANES), F32),
                        pltpu.SMEM((8,), jnp.int32)],
        compiler_params=_params("parallel", "arbitrary"),
        name="moe",
    )(h2, cb, x1, wg, wu, wd)


def _round_up(n, m):
    return (n + m - 1) // m * m


def _layer(x, mk, mv, past_k, past_v, past_ki, s0, w):
    b, t, d = x.shape
    n = b * t
    x2 = x.reshape(n, d)
    p = _norm_proj(x2, w["attn_g"], w["w_cat"], min(1024, n), 1024)
    ka, va, ki = _kv_post(p, w["k_g"], w["ik_g"], min(512, n))

    k3 = ka.reshape(b, t, H_KV * HD_A)
    v3 = va.reshape(b, t, H_KV * HD_A)
    ki3 = ki.reshape(b, t, D_IDX)
    past = 0
    if past_k is not None:
        past = past_k.shape[1]
        k3 = jnp.concatenate([past_k.reshape(b, past, -1), k3], axis=1)
        v3 = jnp.concatenate([past_v.reshape(b, past, -1), v3], axis=1)
        ki3 = jnp.concatenate([past_ki, ki3], axis=1)
    s_valid = past + t
    tk = 512
    pad = ((0, 0), (0, _round_up(s_valid, tk) - s_valid), (0, 0))
    k_rows = jnp.pad(k3.astype(BF16), pad)
    ki2 = jnp.pad(jnp.concatenate([ki3, ki3], axis=-1).astype(BF16), pad)
    dsa_args = dict(b=b, t=t, tk=tk, s_valid=s_valid, past=past, top_k=min(TOPK_MAX, s_valid // 4))
    if t % DSA_TQ == 0:
        vt_all = jnp.pad(v3.astype(BF16), pad).reshape(b, -1, H_KV, HD_A).transpose(0, 2, 3, 1)
        oa = _dsa_t(p, k_rows, vt_all, ki2, w["q_g"], tq=DSA_TQ, **dsa_args)
    else:
        k_all = k_rows.reshape(b, -1, H_KV, HD_A).transpose(0, 2, 3, 1)
        ones = jnp.ones((b, s_valid, HD_A), BF16)
        v_parts = []
        for g in range(H_KV):
            v_parts += [v3[..., g * HD_A:(g + 1) * HD_A].astype(BF16), ones]
        v_all = jnp.pad(jnp.concatenate(v_parts, axis=-1), pad)
        oa = _dsa(p, k_all, v_all, ki2.transpose(0, 2, 1), w["q_g"], tq=t, **dsa_args)

    chunk = CHUNK if t % CHUNK == 0 else t
    tt = min(512, t)
    og, s_new = _gla(p, w["w_up"], w["b_up"], w["gla_g"], s0, b=b, t=t, tt=tt, chunk=chunk)
    om = _mem_attn(p, mk, mv, w["mq_g"], b=b, t=t, tt=tt)
    x1, h2, cb = _merge(x2, oa, og, om, p, w["w_a"], w["w_g"], w["w_m"], w["w_o"],
                        w["ffn_g"], w["w_r"], w["b_r"], min(256, n))
    y = _moe(h2, cb, x1, w["w_eg"], w["w_eu"], w["w_ed"], min(1024, n))
    return (y.reshape(b, t, d), ka.reshape(b, t, H_KV, HD_A), va.reshape(b, t, H_KV, HD_A),
            ki.reshape(b, t, D_IDX), s_new)


def _prep_weights(l, attn_norm_g, w_in, q_norm_g, k_norm_g, idx_k_norm_g, w_gla_gate_up,
                  b_gla_gate_up, gla_out_norm_g, mem_q_norm_g, w_branch_a, w_branch_g,
                  w_branch_m, w_out, ffn_norm_g, w_group_router, b_group_router,
                  w_expert_router, b_expert_router, w_exp_gate, w_exp_up, w_exp_down):
    d = D_MODEL
    wi = w_in[l]
    cols = [wi[:, _SRC[name][0]:_SRC[name][1]] for name in _DST_ORDER]
    used = sum(c.shape[1] for c in cols)
    w_cat = jnp.concatenate(cols + [jnp.zeros((d, C_PAD - used), F32)], axis=1).astype(BF16)
    pad_r = LANES - N_EXPERTS - N_GROUPS
    w_r = jnp.concatenate([w_expert_router[l], w_group_router[l], jnp.zeros((d, pad_r), F32)], axis=1)
    b_r = jnp.concatenate([b_expert_router[l], b_group_router[l], jnp.zeros((pad_r,), F32)])
    return dict(
        attn_g=attn_norm_g[l][None], w_cat=w_cat, q_g=q_norm_g[l][None], k_g=k_norm_g[l][None],
        ik_g=idx_k_norm_g[l][None],
        w_up=w_gla_gate_up[l].astype(BF16), b_up=b_gla_gate_up[l][None],
        gla_g=gla_out_norm_g[l][None],
        mq_g=mem_q_norm_g[l][None],
        w_a=w_branch_a[l].astype(BF16), w_g=w_branch_g[l].astype(BF16),
        w_m=w_branch_m[l].astype(BF16), w_o=w_out[l].astype(BF16),
        ffn_g=ffn_norm_g[l][None], w_r=w_r.astype(BF16), b_r=b_r[None],
        w_eg=w_exp_gate[l].astype(BF16), w_eu=w_exp_up[l].astype(BF16),
        w_ed=w_exp_down[l].astype(BF16))


def kernel(x_prompt, x_sample, mem_prompt, cache_k, cache_v, cache_idx_k, state_gla, cache_mem_k, cache_mem_v, attn_norm_g, w_in, q_norm_g, k_norm_g, idx_k_norm_g, w_gla_gate_up, b_gla_gate_up, gla_out_norm_g, mem_norm_g, w_mem_kv, mem_q_norm_g, mem_k_norm_g, w_branch_a, w_branch_g, w_branch_m, w_out, ffn_norm_g, w_group_router, b_group_router, w_expert_router, b_expert_router, w_exp_gate, w_exp_up, w_exp_down):
    depth = w_in.shape[0]
    y_p, y_s = x_prompt, x_sample
    outs = [[] for _ in range(10)]
    for l in range(depth):
        w = _prep_weights(l, attn_norm_g, w_in, q_norm_g, k_norm_g, idx_k_norm_g, w_gla_gate_up,
                          b_gla_gate_up, gla_out_norm_g, mem_q_norm_g, w_branch_a, w_branch_g,
                          w_branch_m, w_out, ffn_norm_g, w_group_router, b_group_router,
                          w_expert_router, b_expert_router, w_exp_gate, w_exp_up, w_exp_down)
        bp, n_mem, d = mem_prompt.shape
        mk_p, mv_p = _mem_kv(mem_prompt.reshape(bp * n_mem, d), mem_norm_g[l][None],
                             w_mem_kv[l].astype(BF16), mem_k_norm_g[l][None], min(256, bp * n_mem))
        mk_p = mk_p.reshape(bp, n_mem, H_M, HD_M)
        mv_p = mv_p.reshape(bp, n_mem, H_M, HD_M)
        s0 = jnp.zeros((bp, H_G, DK_G, DV_G), F32)
        y_p, k_new, v_new, ki_new, s_new = _layer(
            y_p, mk_p.reshape(bp, n_mem, -1).astype(BF16), mv_p.reshape(bp, n_mem, -1).astype(BF16),
            None, None, None, s0, w)
        for lst, val in zip(outs[:6], (k_new, v_new, ki_new, s_new, mk_p, mv_p)):
            lst.append(val)
        bs = x_sample.shape[0]
        y_s, k_new, v_new, ki_new, s_new = _layer(
            y_s, cache_mem_k[l].reshape(bs, n_mem, -1).astype(BF16),
            cache_mem_v[l].reshape(bs, n_mem, -1).astype(BF16),
            cache_k[l], cache_v[l], cache_idx_k[l], state_gla[l], w)
        for lst, val in zip(outs[6:], (k_new, v_new, ki_new, s_new)):
            lst.append(val)
    return (y_p, y_s) + tuple(jnp.stack(o) for o in outs)
```

```python
import functools

import numpy as np
import jax
import jax.numpy as jnp
from jax import lax
from jax.experimental import pallas as pl
from jax.experimental.pallas import tpu as pltpu

F32 = jnp.float32
BF16 = jnp.bfloat16

D_MODEL = 1024
CHUNK = 64
CHUNK_SHIFT = 6
TOPK_MAX = 256
H_A, HD_A, H_KV = 8, 128, 2
H_IDX, D_IDX = 16, 64
H_G, DK_G, DV_G = 4, 128, 256
GATE_RANK, GATE_TEMP = 16, 16.0
H_M, HD_M = 4, 256
N_GROUPS, E_PER_GROUP = 4, 8
N_EXPERTS = N_GROUPS * E_PER_GROUP
D_EXPERT = 256
EPS = 1e-6
NEG_INF = -1e30
LANES = 128
LOG2E = 1.4426950408889634
GROUP_LANE = LANES - 1
COUNT_ROWS = 128
DSA_TQ = 128
DSA_TK_ATT = 512
MOE_SUB = 128
MOE_E_STEP = 4

QA_OFF, QI_OFF, QG_OFF, KG_OFF, VG_OFF = 0, 1024, 2048, 2560, 3072
GO_OFF, QM_OFF, GB_OFF, KA_OFF, VA_OFF, TAIL_OFF = 4096, 5120, 6144, 9216, 9472, 9728
TAIL_KI, TAIL_WI, TAIL_GLR = 0, 64, 80
C_PAD = 10240
_SRC = dict(qa=(0, 1024), ka=(1024, 1280), va=(1280, 1536), qi=(1536, 2560), ki=(2560, 2624),
            wi=(2624, 2640), qg=(2640, 3152), kg=(3152, 3664), vg=(3664, 4688), glr=(4688, 4704),
            go=(4704, 5728), qm=(5728, 6752), gb=(6752, 9824))
_DST_ORDER = ("qa", "qi", "qg", "kg", "vg", "go", "qm", "gb", "ka", "va", "ki", "wi", "glr")

INT_MIN = -2 ** 31
_KEY_VALID = int(np.float32(NEG_INF * 0.5).view(np.int32)) ^ 0x7FFFFFFF
if _KEY_VALID >= 2 ** 31:
    _KEY_VALID -= 2 ** 32

VMEM_LIMIT = 56 * 1024 * 1024


def _params(*sem):
    return pltpu.CompilerParams(dimension_semantics=sem, vmem_limit_bytes=VMEM_LIMIT)


def _rms(x, g):
    return x * lax.rsqrt(jnp.mean(x * x, axis=-1, keepdims=True) + EPS) * g


def _dot(a, b):
    return jnp.dot(a, b, preferred_element_type=F32)


def _dot_nt(a, b):
    return lax.dot_general(a, b, (((1,), (1,)), ((), ())), preferred_element_type=F32)


def _dot_tn(a, b):
    return lax.dot_general(a, b, (((0,), (0,)), ((), ())), preferred_element_type=F32)


def _norm_proj_kernel(x_ref, g_ref, w_ref, o_ref, h_ref):
    @pl.when(pl.program_id(1) == 0)
    def _():
        h_ref[...] = _rms(x_ref[...], g_ref[...]).astype(BF16)

    o_ref[...] = _dot(h_ref[...], w_ref[...])


def _norm_proj(x, g, w, tm, tn):
    n, d = x.shape
    c = w.shape[1]
    return pl.pallas_call(
        _norm_proj_kernel,
        grid=(n // tm, c // tn),
        in_specs=[pl.BlockSpec((tm, d), lambda i, j: (i, 0)),
                  pl.BlockSpec((1, d), lambda i, j: (0, 0)),
                  pl.BlockSpec((d, tn), lambda i, j: (0, j))],
        out_specs=pl.BlockSpec((tm, tn), lambda i, j: (i, j)),
        out_shape=jax.ShapeDtypeStruct((n, c), F32),
        scratch_shapes=[pltpu.VMEM((tm, d), BF16)],
        compiler_params=_params("parallel", "arbitrary"),
        name="norm_proj",
    )(x, g, w)


def _kv_post_kernel(kv_ref, tail_ref, kg_ref, ig_ref, k_ref, v_ref, ki_ref):
    kv = kv_ref[...]
    for h in range(H_KV):
        k_ref[:, h * HD_A:(h + 1) * HD_A] = _rms(kv[:, h * HD_A:(h + 1) * HD_A], kg_ref[...])
    v_ref[...] = kv[:, H_KV * HD_A:]
    ki_ref[...] = _rms(tail_ref[:, TAIL_KI:TAIL_KI + D_IDX], ig_ref[...])


def _kv_post(p, k_g, ik_g, tm):
    n = p.shape[0]
    w = 2 * H_KV * HD_A
    return pl.pallas_call(
        _kv_post_kernel,
        grid=(n // tm,),
        in_specs=[pl.BlockSpec((tm, w), lambda i: (i, KA_OFF // w)),
                  pl.BlockSpec((tm, LANES), lambda i: (i, TAIL_OFF // LANES)),
                  pl.BlockSpec((1, HD_A), lambda i: (0, 0)),
                  pl.BlockSpec((1, D_IDX), lambda i: (0, 0))],
        out_specs=[pl.BlockSpec((tm, H_KV * HD_A), lambda i: (i, 0)),
                   pl.BlockSpec((tm, H_KV * HD_A), lambda i: (i, 0)),
                   pl.BlockSpec((tm, D_IDX), lambda i: (i, 0))],
        out_shape=[jax.ShapeDtypeStruct((n, H_KV * HD_A), F32),
                   jax.ShapeDtypeStruct((n, H_KV * HD_A), F32),
                   jax.ShapeDtypeStruct((n, D_IDX), F32)],
        compiler_params=_params("parallel"),
        name="kv_post",
    )(p, p, k_g, ik_g)


def _to_key(x):
    x = jnp.where(x == 0.0, 0.0, x)
    b = lax.bitcast_convert_type(x, jnp.int32)
    return jnp.where(b < 0, b ^ 0x7FFFFFFF, b)


def _dsa_kernel(qa_ref, qi_ref, tail_ref, k_ref, v_ref, ki_ref, qg_ref, o_ref,
                key_ref, m_ref, acc_ref, *, tq, tk, s_valid, past, top_k):
    i = pl.program_id(1)
    q0 = past + i * tq
    lim = jnp.minimum((((q0 + tq - 1) >> CHUNK_SHIFT) + 1) << CHUNK_SHIFT, s_valid)
    n_tiles = (lim + tk - 1) // tk
    lane = lax.broadcasted_iota(jnp.int32, (tq, LANES), 1)
    q_chunk = (q0 + lax.broadcasted_iota(jnp.int32, (tq, 1), 0)) >> CHUNK_SHIFT

    qi = qi_ref[...]
    qi_h = []
    for j in range(H_IDX // 2):
        pair = qi[:, j * LANES:(j + 1) * LANES] * (D_IDX ** -0.5)
        qi_h.append(jnp.where(lane < D_IDX, pair, 0.0).astype(BF16))
        qi_h.append(jnp.where(lane >= D_IDX, pair, 0.0).astype(BF16))
    wi = tail_ref[:, TAIL_WI:TAIL_WI + H_IDX] * (H_IDX ** -0.5)

    def tile_off(j):
        return pl.multiple_of(j * tk, tk)

    def score_tile(j, carry):
        off = tile_off(j)
        kt = ki_ref[:, pl.ds(off, tk)]
        acc = jnp.zeros((tq, tk), F32)
        for h in range(H_IDX):
            acc = acc + jnp.maximum(_dot(qi_h[h], kt), 0.0) * wi[:, h:h + 1]
        kpos = off + lax.broadcasted_iota(jnp.int32, (1, tk), 1)
        adm = ((kpos >> CHUNK_SHIFT) <= q_chunk) & (kpos < s_valid)
        key_ref[:, pl.ds(off, tk)] = _to_key(jnp.where(adm, acc, NEG_INF))
        return carry

    lax.fori_loop(0, n_tiles, score_tile, 0)

    rb = min(tq, COUNT_ROWS)

    def count(cmp, thr):
        outs = []
        for r0 in range(0, tq, rb):
            th = thr[r0:r0 + rb]

            def body(j, c, r0=r0, th=th):
                m = jnp.where(cmp(key_ref[r0:r0 + rb, pl.ds(tile_off(j), tk)], th), 1.0, 0.0)
                part = m[:, 0:LANES]
                for u in range(1, tk // LANES):
                    part = part + m[:, u * LANES:(u + 1) * LANES]
                return c + part

            c = lax.fori_loop(0, n_tiles, body, jnp.zeros((rb, LANES), F32))
            outs.append(jnp.sum(c, axis=-1, keepdims=True))
        return outs[0] if len(outs) == 1 else jnp.concatenate(outs, axis=0)

    ge = lambda a, b: a >= b
    gt = lambda a, b: a > b

    def bit_step(it, v):
        trial = v ^ lax.shift_left(jnp.int32(1), 31 - it)
        return jnp.where(count(ge, trial) >= float(top_k), trial, v)

    v = lax.fori_loop(0, 32, bit_step, jnp.full((tq, 1), INT_MIN, jnp.int32))

    excess = (count(ge, v) > float(top_k)) & (v > _KEY_VALID)

    @pl.when(jnp.max(jnp.where(excess, 1.0, 0.0)) > 0.0)
    def _():
        need = float(top_k) - count(gt, v)
        r_i = lax.broadcasted_iota(jnp.int32, (tk, tk), 0)
        c_i = lax.broadcasted_iota(jnp.int32, (tk, tk), 1)
        before = jnp.where(r_i < c_i, 1.0, 0.0).astype(BF16)

        def body(j, seen):
            off = tile_off(j)
            kt = key_ref[:, pl.ds(off, tk)]
            eq = kt == v
            eqf = jnp.where(eq, 1.0, 0.0)
            rank = seen + _dot(eqf.astype(BF16), before)
            key_ref[:, pl.ds(off, tk)] = jnp.where(eq & (rank >= need), v - 1, kt)
            return seen + jnp.sum(eqf, axis=-1, keepdims=True)

        lax.fori_loop(0, n_tiles, body, jnp.zeros((tq, 1), F32))

    qa = qa_ref[...]
    rep = H_A // H_KV
    qs = []
    for g in range(H_KV):
        heads = [(_rms(qa[:, h * HD_A:(h + 1) * HD_A], qg_ref[...]) * (HD_A ** -0.5 * LOG2E)).astype(BF16)
                 for h in range(g * rep, (g + 1) * rep)]
        qs.append(jnp.concatenate(heads, axis=0))
    m_ref[...] = jnp.full(m_ref.shape, NEG_INF, F32)
    acc_ref[...] = jnp.zeros(acc_ref.shape, F32)

    def att_tile(j, carry):
        off = tile_off(j)
        kt_key = key_ref[:, pl.ds(off, tk)]
        bias = jnp.where((kt_key >= v) & (kt_key > _KEY_VALID), 0.0, NEG_INF)
        logits = [_dot(qs[g], k_ref[g, :, pl.ds(off, tk)]) for g in range(H_KV)]
        for g in range(H_KV):
            vt = v_ref[pl.ds(off, tk), 2 * g * HD_A:2 * (g + 1) * HD_A]
            s = (logits[g].reshape(rep, tq, tk) + bias[None]).reshape(rep * tq, tk)
            m_old = m_ref[g]
            m_new = jnp.maximum(m_old, jnp.max(s, axis=-1, keepdims=True))
            m_sub = jnp.where(m_new > NEG_INF * 0.5, m_new, -NEG_INF)
            p = jnp.concatenate([jnp.exp2(s[:, u * LANES:(u + 1) * LANES] - m_sub)
                                 for u in range(tk // LANES)], axis=1)
            alpha = jnp.exp2(m_old - m_new)
            pv = _dot(p.astype(BF16), vt)
            for u in range(2):
                cols = slice(u * HD_A, (u + 1) * HD_A)
                acc_ref[g, :, cols] = alpha * acc_ref[g, :, cols] + pv[:, cols]
            m_ref[g] = m_new
        return carry

    lax.fori_loop(0, n_tiles, att_tile, 0)
    for g in range(H_KV):
        out = acc_ref[g, :, 0:HD_A] / acc_ref[g, :, HD_A:2 * HD_A]
        for r in range(rep):
            h = g * rep + r
            o_ref[:, h * HD_A:(h + 1) * HD_A] = out[r * tq:(r + 1) * tq].astype(o_ref.dtype)


def _dsa(p, k_all, v_all, ki2, q_g, *, b, t, tq, tk, s_valid, past, top_k):
    nq = t // tq
    s_pad = k_all.shape[-1]
    wq = H_A * HD_A
    kern = functools.partial(_dsa_kernel, tq=tq, tk=tk, s_valid=s_valid, past=past, top_k=top_k)
    return pl.pallas_call(
        kern,
        grid=(b, nq),
        in_specs=[pl.BlockSpec((tq, wq), lambda bb, i: (bb * nq + i, QA_OFF // wq)),
                  pl.BlockSpec((tq, wq), lambda bb, i: (bb * nq + i, QI_OFF // wq)),
                  pl.BlockSpec((tq, LANES), lambda bb, i: (bb * nq + i, TAIL_OFF // LANES)),
                  pl.BlockSpec((None, H_KV, HD_A, s_pad), lambda bb, i: (bb, 0, 0, 0)),
                  pl.BlockSpec((None, s_pad, 2 * H_KV * HD_A), lambda bb, i: (bb, 0, 0)),
                  pl.BlockSpec((None, LANES, s_pad), lambda bb, i: (bb, 0, 0)),
                  pl.BlockSpec((1, HD_A), lambda bb, i: (0, 0))],
        out_specs=pl.BlockSpec((tq, wq), lambda bb, i: (bb * nq + i, 0)),
        out_shape=jax.ShapeDtypeStruct((b * t, wq), BF16),
        scratch_shapes=[pltpu.VMEM((tq, s_pad), jnp.int32),
                        pltpu.VMEM((H_KV, H_A // H_KV * tq, LANES), F32),
                        pltpu.VMEM((H_KV, H_A // H_KV * tq, 2 * HD_A), F32)],
        compiler_params=_params("parallel", "arbitrary"),
        name="dsa",
    )(p, p, p, k_all, v_all, ki2, q_g)


def _fold8(x, op, rows=8):
    parts = [x[r:r + rows] for r in range(0, x.shape[0], rows)]
    while len(parts) > 1:
        nxt = [op(parts[a], parts[a + 1]) for a in range(0, len(parts) - 1, 2)]
        if len(parts) % 2:
            nxt.append(parts[-1])
        parts = nxt
    return parts[0]


def _rowsum8(x):
    return _fold8(x, jnp.add)


def _rowmax8(x):
    return _fold8(x, jnp.maximum)


def _dsa_t_kernel(qa_ref, qi_ref, tail_ref, k_ref, vt_ref, ki_ref, qg_ref, o_ref,
                  key_ref, hi_ref, m_ref, l_ref, acc_ref, *, tq, tk, s_valid, past, top_k):
    i = pl.program_id(1)
    q0 = past + i * tq
    lim = jnp.minimum((((q0 + tq - 1) >> CHUNK_SHIFT) + 1) << CHUNK_SHIFT, s_valid)
    n_tiles = (lim + tk - 1) // tk
    q_chunk = (q0 + lax.broadcasted_iota(jnp.int32, (1, tq), 1)) >> CHUNK_SHIFT
    sub = lax.broadcasted_iota(jnp.int32, (LANES, tq), 0)

    qi = qi_ref[...]
    qi_t = []
    for j in range(H_IDX // 2):
        pair_t = (qi[:, j * LANES:(j + 1) * LANES] * (D_IDX ** -0.5)).T
        qi_t.append(jnp.concatenate([jnp.where(sub < D_IDX, pair_t, 0.0),
                                     jnp.where(sub >= D_IDX, pair_t, 0.0)], axis=1).astype(BF16))
    wi_t = tail_ref[...].T * (H_IDX ** -0.5)

    def tile_off(j):
        return pl.multiple_of(j * tk, tk)

    def score_tile(j, carry):
        off = tile_off(j)
        kt = ki_ref[pl.ds(off, tk), :]
        acc = jnp.zeros((tk, tq), F32)
        for jj in range(H_IDX // 2):
            r = jnp.maximum(_dot(kt, qi_t[jj]), 0.0)
            h = TAIL_WI + 2 * jj
            acc = acc + r[:, 0:tq] * wi_t[h:h + 1, :] + r[:, tq:2 * tq] * wi_t[h + 1:h + 2, :]
        kpos = off + lax.broadcasted_iota(jnp.int32, (tk, 1), 0)
        adm = ((kpos >> CHUNK_SHIFT) <= q_chunk) & (kpos < s_valid)
        keyed = _to_key(jnp.where(adm, acc, NEG_INF))
        key_ref[pl.ds(off, tk), :] = keyed
        hi_ref[pl.ds(off, tk), :] = (keyed >> 16).astype(jnp.int16)
        return carry

    lax.fori_loop(0, n_tiles, score_tile, 0)

    def count(cmp, thr):
        def body(j, c):
            m = jnp.where(cmp(key_ref[pl.ds(tile_off(j), tk), :], thr), 1.0, 0.0)
            return c + _rowsum8(m)
        c = lax.fori_loop(0, n_tiles, body, jnp.zeros((8, tq), F32))
        return jnp.sum(c, axis=0, keepdims=True)

    def count_hi(thr):
        def body(j, c):
            m = jnp.where(hi_ref[pl.ds(tile_off(j), tk), :] >= thr, jnp.int16(1), jnp.int16(0))
            return c + _fold8(m, jnp.add, rows=16)
        c = lax.fori_loop(0, n_tiles, body, jnp.zeros((16, tq), jnp.int16))
        return jnp.sum(c.astype(F32), axis=0, keepdims=True)

    ge = lambda a, b: a >= b
    gt = lambda a, b: a > b

    def hi_step(it, vh):
        trial = vh + lax.shift_left(jnp.int32(1), 15 - it)
        return jnp.where(count_hi(trial.astype(jnp.int16)) >= float(top_k), trial, vh)

    def bit_step(it, v):
        trial = v ^ lax.shift_left(jnp.int32(1), 31 - it)
        return jnp.where(count(ge, trial) >= float(top_k), trial, v)

    vh = lax.fori_loop(0, 16, hi_step, jnp.full((1, tq), -2 ** 15, jnp.int32))
    v = lax.fori_loop(16, 32, bit_step, vh << 16)

    excess = (count(ge, v) > float(top_k)) & (v > _KEY_VALID)

    @pl.when(jnp.max(jnp.where(excess, 1.0, 0.0)) > 0.0)
    def _():
        need = float(top_k) - count(gt, v)
        r_i = lax.broadcasted_iota(jnp.int32, (tk, tk), 0)
        c_i = lax.broadcasted_iota(jnp.int32, (tk, tk), 1)
        before = jnp.where(c_i < r_i, 1.0, 0.0).astype(BF16)

        def body(j, seen):
            rows = pl.ds(tile_off(j), tk)
            kt = key_ref[rows, :]
            eq = kt == v
            eqf = jnp.where(eq, 1.0, 0.0)
            rank = seen + _dot(before, eqf.astype(BF16))
            key_ref[rows, :] = jnp.where(eq & (rank >= need), v - 1, kt)
            return seen + jnp.sum(_rowsum8(eqf), axis=0, keepdims=True)

        lax.fori_loop(0, n_tiles, body, jnp.zeros((1, tq), F32))

    rep = H_A // H_KV
    qa = qa_ref[...]
    q_t = []
    for g in range(H_KV):
        heads = [(_rms(qa[:, h * HD_A:(h + 1) * HD_A], qg_ref[...]) * (HD_A ** -0.5 * LOG2E)).T
                 for h in range(g * rep, (g + 1) * rep)]
        q_t.append(jnp.concatenate(heads, axis=1).astype(BF16))
    m_ref[...] = jnp.full(m_ref.shape, NEG_INF, F32)
    l_ref[...] = jnp.zeros(l_ref.shape, F32)
    acc_ref[...] = jnp.zeros(acc_ref.shape, F32)

    half = rep // 2
    ta = min(DSA_TK_ATT, tk)
    n_att = (lim + ta - 1) // ta

    def att_tile(j, carry):
        off = pl.multiple_of(j * ta, ta)
        rows = pl.ds(off, ta)
        kt_key = key_ref[rows, :]
        bias = jnp.where((kt_key >= v) & (kt_key > _KEY_VALID), 0.0, NEG_INF)
        logits = [_dot(k_ref[rows, g * HD_A:(g + 1) * HD_A], q_t[g]) for g in range(H_KV)]
        for g in range(H_KV):
            vt = vt_ref[g, :, rows]
            for r0 in range(0, rep, half):
                blk = slice(r0 * tq, (r0 + half) * tq)
                m_old = m_ref[g, :, blk]
                ps, ms, ls = [], [], []
                for r in range(r0, r0 + half):
                    s = logits[g][:, r * tq:(r + 1) * tq] + bias
                    m_o = m_old[:, (r - r0) * tq:(r - r0 + 1) * tq]
                    m_n = jnp.maximum(m_o, jnp.max(_rowmax8(s), axis=0, keepdims=True))
                    p = jnp.exp2(s - jnp.where(m_n > NEG_INF * 0.5, m_n, -NEG_INF)[0:1])
                    ps.append(p.astype(BF16))
                    ms.append(m_n)
                    ls.append(jnp.sum(_rowsum8(p), axis=0, keepdims=True))
                m_new = jnp.concatenate(ms, axis=1)
                alpha = jnp.exp2(m_old - m_new)
                l_ref[g, :, blk] = alpha * l_ref[g, :, blk] + jnp.concatenate(ls, axis=1)
                pv = _dot(vt, jnp.concatenate(ps, axis=1))
                acc_ref[g, :, blk] = alpha[0:1] * acc_ref[g, :, blk] + pv
                m_ref[g, :, blk] = m_new
        return carry

    lax.fori_loop(0, n_att, att_tile, 0)
    for g in range(H_KV):
        out = acc_ref[g] / l_ref[g][0:1]
        for r in range(rep):
            h = g * rep + r
            o_ref[:, h * HD_A:(h + 1) * HD_A] = out[:, r * tq:(r + 1) * tq].T.astype(o_ref.dtype)


def _dsa_t(p, k_all, vt_all, ki2, q_g, *, b, t, tq, tk, s_valid, past, top_k):
    nq = t // tq
    s_pad = k_all.shape[1]
    wq = H_A * HD_A
    rep = H_A // H_KV
    kern = functools.partial(_dsa_t_kernel, tq=tq, tk=tk, s_valid=s_valid, past=past, top_k=top_k)
    return pl.pallas_call(
        kern,
        grid=(b, nq),
        in_specs=[pl.BlockSpec((tq, wq), lambda bb, i: (bb * nq + i, QA_OFF // wq)),
                  pl.BlockSpec((tq, wq), lambda bb, i: (bb * nq + i, QI_OFF // wq)),
                  pl.BlockSpec((tq, LANES), lambda bb, i: (bb * nq + i, TAIL_OFF // LANES)),
                  pl.BlockSpec((None, s_pad, H_KV * HD_A), lambda bb, i: (bb, 0, 0)),
                  pl.BlockSpec((None, H_KV, HD_A, s_pad), lambda bb, i: (bb, 0, 0, 0)),
                  pl.BlockSpec((None, s_pad, LANES), lambda bb, i: (bb, 0, 0)),
                  pl.BlockSpec((1, HD_A), lambda bb, i: (0, 0))],
        out_specs=pl.BlockSpec((tq, wq), lambda bb, i: (bb * nq + i, 0)),
        out_shape=jax.ShapeDtypeStruct((b * t, wq), BF16),
        scratch_shapes=[pltpu.VMEM((s_pad, tq), jnp.int32),
                        pltpu.VMEM((s_pad, tq), jnp.int16),
                        pltpu.VMEM((H_KV, 8, rep * tq), F32),
                        pltpu.VMEM((H_KV, 8, rep * tq), F32),
                        pltpu.VMEM((H_KV, HD_A, rep * tq), F32)],
        compiler_params=_params("parallel", "arbitrary"),
        name="dsa_t",
    )(p, p, p, k_all, vt_all, ki2, q_g)


def _gla_kernel(q_ref, k_ref, v_ref, go_ref, tail_ref, wup_ref, bup_ref, ng_ref, s0_ref,
                o_ref, s_ref, la_ref, *, tt, chunk):
    @pl.when(pl.program_id(1) == 0)
    def _():
        s_ref[...] = s0_ref[...]

    glr = tail_ref[:, TAIL_GLR:TAIL_GLR + GATE_RANK].astype(BF16)
    z = _dot(glr, wup_ref[...]) + bup_ref[...]
    la_ref[...] = (jnp.minimum(z, 0.0) - jnp.log1p(jnp.exp(-jnp.abs(z)))) / GATE_TEMP

    width = H_G * DK_G
    row = lax.broadcasted_iota(jnp.int32, (chunk, width), 0)
    causal = (lax.broadcasted_iota(jnp.int32, (chunk, chunk), 0)
              >= lax.broadcasted_iota(jnp.int32, (chunk, chunk), 1))
    eye = (lax.broadcasted_iota(jnp.int32, (DK_G, DK_G), 0)
           == lax.broadcasted_iota(jnp.int32, (DK_G, DK_G), 1))

    def body(c, carry):
        off = pl.multiple_of(c * chunk, chunk)
        rows = pl.ds(off, chunk)
        bcum = la_ref[rows, :]
        step = 1
        while step < chunk:
            bcum = bcum + jnp.where(row >= step, pltpu.roll(bcum, step, axis=0), 0.0)
            step *= 2
        b_last = bcum[chunk - 1:chunk, :]
        q = q_ref[rows, :] * (DK_G ** -0.5)
        k = k_ref[rows, :]
        qe = (q * jnp.exp(bcum)).astype(BF16)
        ke = (k * jnp.exp(-bcum)).astype(BF16)
        kd = (k * jnp.exp(b_last - bcum)).astype(BF16)
        decay = jnp.exp(b_last)
        hk = [slice(h * DK_G, (h + 1) * DK_G) for h in range(H_G)]
        hv = [slice(h * DV_G, (h + 1) * DV_G) for h in range(H_G)]
        att = [jnp.where(causal, _dot_nt(qe[:, hk[h]], ke[:, hk[h]]), 0.0).astype(BF16)
               for h in range(H_G)]
        vv = [v_ref[rows, hv[h]].astype(BF16) for h in range(H_G)]
        state = [s_ref[h] for h in range(H_G)]
        inter = [_dot(qe[:, hk[h]], state[h].astype(BF16)) for h in range(H_G)]
        intra = [_dot(att[h], vv[h]) for h in range(H_G)]
        upd = [_dot_tn(kd[:, hk[h]], vv[h]) for h in range(H_G)]
        for h in range(H_G):
            decay_col = jnp.sum(jnp.where(eye, jnp.broadcast_to(decay[:, hk[h]], (DK_G, DK_G)), 0.0),
                                axis=1, keepdims=True)
            s_ref[h] = state[h] * decay_col + upd[h]
            go = go_ref[rows, hv[h]]
            o_ref[rows, hv[h]] = (_rms(intra[h] + inter[h], ng_ref[...])
                                  * (go * jax.nn.sigmoid(go))).astype(o_ref.dtype)
        return carry

    lax.fori_loop(0, tt // chunk, body, 0)


def _gla(p, w_up, b_up, n_g, s0, *, b, t, tt, chunk):
    nt = t // tt
    kern = functools.partial(_gla_kernel, tt=tt, chunk=chunk)
    wk, wv = H_G * DK_G, H_G * DV_G
    rowblk = lambda w, off: pl.BlockSpec((tt, w), lambda bb, i: (bb * nt + i, off // w))
    state = pl.BlockSpec((None, H_G, DK_G, DV_G), lambda bb, i: (bb, 0, 0, 0))
    return pl.pallas_call(
        kern,
        grid=(b, nt),
        in_specs=[rowblk(wk, QG_OFF), rowblk(wk, KG_OFF), rowblk(wv, VG_OFF), rowblk(wv, GO_OFF),
                  rowblk(LANES, TAIL_OFF),
                  pl.BlockSpec((GATE_RANK, wk), lambda bb, i: (0, 0)),
                  pl.BlockSpec((1, wk), lambda bb, i: (0, 0)),
                  pl.BlockSpec((1, DV_G), lambda bb, i: (0, 0)),
                  state],
        out_specs=[pl.BlockSpec((tt, wv), lambda bb, i: (bb * nt + i, 0)), state],
        out_shape=[jax.ShapeDtypeStruct((b * t, wv), BF16),
                   jax.ShapeDtypeStruct((b, H_G, DK_G, DV_G), F32)],
        scratch_shapes=[pltpu.VMEM((tt, wk), F32)],
        compiler_params=_params("parallel", "arbitrary"),
        name="gla",
    )(p, p, p, p, p, w_up, b_up, n_g, s0)


def _mem_kv_kernel(x_ref, g_ref, w_ref, kg_ref, mk_ref, mv_ref):
    kv = _dot(_rms(x_ref[...], g_ref[...]).astype(BF16), w_ref[...])
    for h in range(H_M):
        mk_ref[:, h * HD_M:(h + 1) * HD_M] = _rms(kv[:, h * HD_M:(h + 1) * HD_M], kg_ref[...])
    mv_ref[...] = kv[:, H_M * HD_M:]


def _mem_kv(mem, g, w, k_g, tm):
    n, d = mem.shape
    wk = H_M * HD_M
    return pl.pallas_call(
        _mem_kv_kernel,
        grid=(n // tm,),
        in_specs=[pl.BlockSpec((tm, d), lambda i: (i, 0)),
                  pl.BlockSpec((1, d), lambda i: (0, 0)),
                  pl.BlockSpec((d, 2 * wk), lambda i: (0, 0)),
                  pl.BlockSpec((1, HD_M), lambda i: (0, 0))],
        out_specs=[pl.BlockSpec((tm, wk), lambda i: (i, 0)),
                   pl.BlockSpec((tm, wk), lambda i: (i, 0))],
        out_shape=[jax.ShapeDtypeStruct((n, wk), F32), jax.ShapeDtypeStruct((n, wk), F32)],
        compiler_params=_params("parallel"),
        name="mem_kv",
    )(mem, g, w, k_g)


def _mem_attn_kernel(q_ref, mk_ref, mv_ref, g_ref, o_ref):
    qm = q_ref[...]
    for h in range(H_M):
        cols = slice(h * HD_M, (h + 1) * HD_M)
        q = (_rms(qm[:, cols], g_ref[...]) * (HD_M ** -0.5)).astype(BF16)
        s = _dot_nt(q, mk_ref[:, cols])
        e = jnp.exp(s - jnp.max(s, axis=-1, keepdims=True))
        prob = e / jnp.sum(e, axis=-1, keepdims=True)
        o_ref[:, cols] = _dot(prob.astype(BF16), mv_ref[:, cols]).astype(o_ref.dtype)


def _mem_attn(p, mk, mv, q_g, *, b, t, tt):
    nt = t // tt
    wq = H_M * HD_M
    n_mem = mk.shape[1]
    return pl.pallas_call(
        _mem_attn_kernel,
        grid=(b, nt),
        in_specs=[pl.BlockSpec((tt, wq), lambda bb, i: (bb * nt + i, QM_OFF // wq)),
                  pl.BlockSpec((None, n_mem, wq), lambda bb, i: (bb, 0, 0)),
                  pl.BlockSpec((None, n_mem, wq), lambda bb, i: (bb, 0, 0)),
                  pl.BlockSpec((1, HD_M), lambda bb, i: (0, 0))],
        out_specs=pl.BlockSpec((tt, wq), lambda bb, i: (bb * nt + i, 0)),
        out_shape=jax.ShapeDtypeStruct((b * t, wq), BF16),
        compiler_params=_params("parallel", "arbitrary"),
        name="mem_attn",
    )(p, mk, mv, q_g)


def _merge_kernel(x_ref, oa_ref, og_ref, om_ref, gb_ref, wa_ref, wg_ref, wm_ref, wo_ref,
                  fg_ref, wr_ref, br_ref, x1_ref, h2_ref, cb_ref):
    d = D_MODEL
    merged = (jax.nn.sigmoid(gb_ref[:, 0:d]) * _dot(oa_ref[...], wa_ref[...])
              + jax.nn.sigmoid(gb_ref[:, d:2 * d]) * _dot(og_ref[...], wg_ref[...])
              + jax.nn.sigmoid(gb_ref[:, 2 * d:3 * d]) * _dot(om_ref[...], wm_ref[...]))
    x1 = x_ref[...] + _dot(merged.astype(BF16), wo_ref[...])
    x1_ref[...] = x1
    hb = _rms(x1, fg_ref[...]).astype(BF16)
    h2_ref[...] = hb

    lg = _dot(hb, wr_ref[...]) + br_ref[...]
    lane = lax.broadcasted_iota(jnp.int32, lg.shape, 1).astype(F32)
    big = 1e9
    gmask = (lane >= N_EXPERTS) & (lane < N_EXPERTS + N_GROUPS)
    gl = jnp.where(gmask, lg, NEG_INF)
    gmax = jnp.max(gl, axis=-1, keepdims=True)
    gidx = jnp.min(jnp.where(gmask & (gl == gmax), lane, big), axis=-1, keepdims=True) - N_EXPERTS
    g_w = 1.0 / jnp.sum(jnp.where(gmask, jnp.exp(gl - gmax), 0.0), axis=-1, keepdims=True)
    lo = gidx * E_PER_GROUP
    emask = (lane >= lo) & (lane < lo + E_PER_GROUP)
    el = jnp.where(emask, lg, NEG_INF)
    ee = jnp.where(emask, jnp.exp(el - jnp.max(el, axis=-1, keepdims=True)), 0.0)
    ep = ee / jnp.sum(ee, axis=-1, keepdims=True)
    p1 = jnp.max(jnp.where(emask, ep, -1.0), axis=-1, keepdims=True)
    i1 = jnp.min(jnp.where(emask & (ep == p1), lane, big), axis=-1, keepdims=True)
    rest = emask & (lane != i1)
    p2 = jnp.max(jnp.where(rest, ep, -1.0), axis=-1, keepdims=True)
    i2 = jnp.min(jnp.where(rest & (ep == p2), lane, big), axis=-1, keepdims=True)
    den = p1 + p2
    cb_ref[...] = (jnp.where(lane == i1, g_w * (p1 / den), 0.0)
                   + jnp.where(lane == i2, g_w * (p2 / den), 0.0)
                   + jnp.where(lane == GROUP_LANE, gidx, 0.0))


def _merge(x, oa, og, om, p, wa, wg, wm, wo, f_g, w_r, b_r, tm):
    n, d = x.shape
    full = lambda shape: pl.BlockSpec(shape, lambda i: (0, 0))
    rowblk = lambda w: pl.BlockSpec((tm, w), lambda i: (i, 0))
    return pl.pallas_call(
        _merge_kernel,
        grid=(n // tm,),
        in_specs=[rowblk(d), rowblk(d), rowblk(d), rowblk(d),
                  pl.BlockSpec((tm, 3 * d), lambda i: (i, GB_OFF // (3 * d))),
                  full((d, d)), full((d, d)), full((d, d)), full((d, d)),
                  full((1, d)), full((d, LANES)), full((1, LANES))],
        out_specs=[rowblk(d), rowblk(d), rowblk(LANES)],
        out_shape=[jax.ShapeDtypeStruct((n, d), F32),
                   jax.ShapeDtypeStruct((n, d), BF16),
                   jax.ShapeDtypeStruct((n, LANES), F32)],
        compiler_params=_params("parallel"),
        name="merge",
    )(x, oa, og, om, p, wa, wg, wm, wo, f_g, w_r, b_r)


def _split3(x):
    hi = x.astype(BF16)
    r1 = x - hi.astype(F32)
    mid = r1.astype(BF16)
    lo = (r1 - mid.astype(F32)).astype(BF16)
    return hi, mid, lo


def _permute_rows(onehot, x):
    hi, mid, lo = _split3(x)
    return _dot(onehot, hi) + _dot(onehot, mid) + _dot(onehot, lo)


def _moe_kernel(h_ref, cb_ref, x1_ref, wg_ref, wu_ref, wd_ref, y_ref,
                hp_ref, cbp_ref, yp_ref, dest_ref, seg_ref):
    e = pl.program_id(1)
    tm = h_ref.shape[0]

    @pl.when(e == 0)
    def _():
        cb = cb_ref[...]
        g_row = cb.T[GROUP_LANE:GROUP_LANE + 1, :]
        grp = lax.broadcasted_iota(jnp.int32, (8, tm), 0).astype(F32)
        member = jnp.where(g_row == grp, 1.0, 0.0)
        r_i = lax.broadcasted_iota(jnp.int32, (tm, tm), 0)
        c_i = lax.broadcasted_iota(jnp.int32, (tm, tm), 1)
        upto = jnp.where(r_i <= c_i, 1.0, 0.0).astype(BF16)
        cnt = _dot(member.astype(BF16), upto)
        tot = cnt[:, tm - 1:tm]
        start = jnp.zeros((1, 1), F32)
        dest = jnp.zeros((1, tm), F32)
        seg_ref[0] = 0
        for j in range(N_GROUPS):
            dest = dest + member[j:j + 1] * (start + cnt[j:j + 1] - 1.0)
            start = start + tot[j:j + 1]
            seg_ref[j + 1] = jnp.sum(start).astype(jnp.int32)
        place = jnp.where(r_i.astype(F32) == dest, 1.0, 0.0).astype(BF16)
        hp_ref[...] = _dot(place, h_ref[...]).astype(BF16)
        cbp_ref[...] = _permute_rows(place, cb)
        dest_ref[...] = jnp.broadcast_to(dest, (LANES, tm)).T
        yp_ref[...] = jnp.zeros(yp_ref.shape, F32)

    sub = min(MOE_SUB, tm)
    n_e = wg_ref.shape[0]
    g = (e * n_e) // E_PER_GROUP
    first = seg_ref[g] // sub
    last = (seg_ref[g + 1] + sub - 1) // sub
    lane = lax.broadcasted_iota(jnp.int32, (sub, LANES), 1)

    def sub_tile(s, carry):
        rows = pl.ds(pl.multiple_of(s * sub, sub), sub)
        h = hp_ref[rows, :]
        cbp = cbp_ref[rows, :]
        gate = [_dot(h, wg_ref[k]) for k in range(n_e)]
        up = [_dot(h, wu_ref[k]) for k in range(n_e)]
        out = None
        for k in range(n_e):
            c = jnp.sum(jnp.where(lane == e * n_e + k, cbp, 0.0), axis=-1, keepdims=True)
            hid = (gate[k] * jax.nn.sigmoid(gate[k])) * up[k] * c
            part = _dot(hid.astype(BF16), wd_ref[k])
            out = part if out is None else out + part
        yp_ref[rows, :] += out
        return carry

    lax.fori_loop(first, last, sub_tile, 0)

    @pl.when(e == pl.num_programs(1) - 1)
    def _():
        col = lax.broadcasted_iota(jnp.int32, (tm, tm), 1).astype(F32)
        back = jnp.where(col == dest_ref[:, 0:1], 1.0, 0.0).astype(BF16)
        y_ref[...] = x1_ref[...] + _permute_rows(back, yp_ref[...])


def _moe(h2, cb, x1, wg, wu, wd, tm):
    n, d = x1.shape
    return pl.pallas_call(
        _moe_kernel,
        grid=(n // tm, N_EXPERTS // MOE_E_STEP),
        in_specs=[pl.BlockSpec((tm, d), lambda i, e: (i, 0)),
                  pl.BlockSpec((tm, LANES), lambda i, e: (i, 0)),
                  pl.BlockSpec((tm, d), lambda i, e: (i, 0)),
                  pl.BlockSpec((MOE_E_STEP, d, D_EXPERT), lambda i, e: (e, 0, 0)),
                  pl.BlockSpec((MOE_E_STEP, d, D_EXPERT), lambda i, e: (e, 0, 0)),
                  pl.BlockSpec((MOE_E_STEP, D_EXPERT, d), lambda i, e: (e, 0, 0))],
        out_specs=pl.BlockSpec((tm, d), lambda i, e: (i, 0)),
        out_shape=jax.ShapeDtypeStruct((n, d), F32),
        scratch_shapes=[pltpu.VMEM((tm, d), BF16),
                        pltpu.VMEM((tm, LANES), F32),
                        pltpu.VMEM((tm, d), F32),
                        pltpu.VMEM((tm, LANES), F32),
                        pltpu.SMEM((8,), jnp.int32)],
        compiler_params=_params("parallel", "arbitrary"),
        name="moe",
    )(h2, cb, x1, wg, wu, wd)


def _round_up(n, m):
    return (n + m - 1) // m * m


def _layer(x, mk, mv, past_k, past_v, past_ki, s0, w):
    b, t, d = x.shape
    n = b * t
    x2 = x.reshape(n, d)
    p = _norm_proj(x2, w["attn_g"], w["w_cat"], min(1024, n), 2048)
    ka, va, ki = _kv_post(p, w["k_g"], w["ik_g"], min(512, n))

    k3 = ka.reshape(b, t, H_KV * HD_A)
    v3 = va.reshape(b, t, H_KV * HD_A)
    ki3 = ki.reshape(b, t, D_IDX)
    past = 0
    if past_k is not None:
        past = past_k.shape[1]
        k3 = jnp.concatenate([past_k.reshape(b, past, -1), k3], axis=1)
        v3 = jnp.concatenate([past_v.reshape(b, past, -1), v3], axis=1)
        ki3 = jnp.concatenate([past_ki, ki3], axis=1)
    s_valid = past + t
    tk = 512
    pad = ((0, 0), (0, _round_up(s_valid, tk) - s_valid), (0, 0))
    k_rows = jnp.pad(k3.astype(BF16), pad)
    ki2 = jnp.pad(jnp.concatenate([ki3, ki3], axis=-1).astype(BF16), pad)
    dsa_args = dict(b=b, t=t, tk=tk, s_valid=s_valid, past=past, top_k=min(TOPK_MAX, s_valid // 4))
    if t % DSA_TQ == 0:
        vt_all = jnp.pad(v3.astype(BF16), pad).reshape(b, -1, H_KV, HD_A).transpose(0, 2, 3, 1)
        oa = _dsa_t(p, k_rows, vt_all, ki2, w["q_g"], tq=DSA_TQ, **dsa_args)
    else:
        k_all = k_rows.reshape(b, -1, H_KV, HD_A).transpose(0, 2, 3, 1)
        ones = jnp.ones((b, s_valid, HD_A), BF16)
        v_parts = []
        for g in range(H_KV):
            v_parts += [v3[..., g * HD_A:(g + 1) * HD_A].astype(BF16), ones]
        v_all = jnp.pad(jnp.concatenate(v_parts, axis=-1), pad)
        oa = _dsa(p, k_all, v_all, ki2.transpose(0, 2, 1), w["q_g"], tq=t, **dsa_args)

    chunk = CHUNK if t % CHUNK == 0 else t
    tt = min(512, t)
    og, s_new = _gla(p, w["w_up"], w["b_up"], w["gla_g"], s0, b=b, t=t, tt=tt, chunk=chunk)
    om = _mem_attn(p, mk, mv, w["mq_g"], b=b, t=t, tt=tt)
    x1, h2, cb = _merge(x2, oa, og, om, p, w["w_a"], w["w_g"], w["w_m"], w["w_o"],
                        w["ffn_g"], w["w_r"], w["b_r"], min(512, n))
    y = _moe(h2, cb, x1, w["w_eg"], w["w_eu"], w["w_ed"], min(1024, n))
    return (y.reshape(b, t, d), ka.reshape(b, t, H_KV, HD_A), va.reshape(b, t, H_KV, HD_A),
            ki.reshape(b, t, D_IDX), s_new)


def _prep_weights(l, attn_norm_g, w_in, q_norm_g, k_norm_g, idx_k_norm_g, w_gla_gate_up,
                  b_gla_gate_up, gla_out_norm_g, mem_q_norm_g, w_branch_a, w_branch_g,
                  w_branch_m, w_out, ffn_norm_g, w_group_router, b_group_router,
                  w_expert_router, b_expert_router, w_exp_gate, w_exp_up, w_exp_down):
    d = D_MODEL
    wi = w_in[l]
    cols = [wi[:, _SRC[name][0]:_SRC[name][1]] for name in _DST_ORDER]
    used = sum(c.shape[1] for c in cols)
    w_cat = jnp.concatenate(cols + [jnp.zeros((d, C_PAD - used), F32)], axis=1).astype(BF16)
    pad_r = LANES - N_EXPERTS - N_GROUPS
    w_r = jnp.concatenate([w_expert_router[l], w_group_router[l], jnp.zeros((d, pad_r), F32)], axis=1)
    b_r = jnp.concatenate([b_expert_router[l], b_group_router[l], jnp.zeros((pad_r,), F32)])
    return dict(
        attn_g=attn_norm_g[l][None], w_cat=w_cat, q_g=q_norm_g[l][None], k_g=k_norm_g[l][None],
        ik_g=idx_k_norm_g[l][None],
        w_up=w_gla_gate_up[l].astype(BF16), b_up=b_gla_gate_up[l][None],
        gla_g=gla_out_norm_g[l][None],
        mq_g=mem_q_norm_g[l][None],
        w_a=w_branch_a[l].astype(BF16), w_g=w_branch_g[l].astype(BF16),
        w_m=w_branch_m[l].astype(BF16), w_o=w_out[l].astype(BF16),
        ffn_g=ffn_norm_g[l][None], w_r=w_r.astype(BF16), b_r=b_r[None],
        w_eg=w_exp_gate[l].astype(BF16), w_eu=w_exp_up[l].astype(BF16),
        w_ed=w_exp_down[l].astype(BF16))


def kernel(x_prompt, x_sample, mem_prompt, cache_k, cache_v, cache_idx_k, state_gla, cache_mem_k, cache_mem_v, attn_norm_g, w_in, q_norm_g, k_norm_g, idx_k_norm_g, w_gla_gate_up, b_gla_gate_up, gla_out_norm_g, mem_norm_g, w_mem_kv, mem_q_norm_g, mem_k_norm_g, w_branch_a, w_branch_g, w_branch_m, w_out, ffn_norm_g, w_group_router, b_group_router, w_expert_router, b_expert_router, w_exp_gate, w_exp_up, w_exp_down):
    depth = w_in.shape[0]
    y_p, y_s = x_prompt, x_sample
    outs = [[] for _ in range(10)]
    for l in range(depth):
        w = _prep_weights(l, attn_norm_g, w_in, q_norm_g, k_norm_g, idx_k_norm_g, w_gla_gate_up,
                          b_gla_gate_up, gla_out_norm_g, mem_q_norm_g, w_branch_a, w_branch_g,
                          w_branch_m, w_out, ffn_norm_g, w_group_router, b_group_router,
                          w_expert_router, b_expert_router, w_exp_gate, w_exp_up, w_exp_down)
        bp, n_mem, d = mem_prompt.shape
        mk_p, mv_p = _mem_kv(mem_prompt.reshape(bp * n_mem, d), mem_norm_g[l][None],
                             w_mem_kv[l].astype(BF16), mem_k_norm_g[l][None], min(256, bp * n_mem))
        mk_p = mk_p.reshape(bp, n_mem, H_M, HD_M)
        mv_p = mv_p.reshape(bp, n_mem, H_M, HD_M)
        s0 = jnp.zeros((bp, H_G, DK_G, DV_G), F32)
        y_p, k_new, v_new, ki_new, s_new = _layer(
            y_p, mk_p.reshape(bp, n_mem, -1).astype(BF16), mv_p.reshape(bp, n_mem, -1).astype(BF16),
            None, None, None, s0, w)
        for lst, val in zip(outs[:6], (k_new, v_new, ki_new, s_new, mk_p, mv_p)):
            lst.append(val)
        bs = x_sample.shape[0]
        y_s, k_new, v_new, ki_new, s_new = _layer(
            y_s, cache_mem_k[l].reshape(bs, n_mem, -1).astype(BF16),
            cache_mem_v[l].reshape(bs, n_mem, -1).astype(BF16),
            cache_k[l], cache_v[l], cache_idx_k[l], state_gla[l], w)
        for lst, val in zip(outs[6:], (k_new, v_new, ki_new, s_new)):
            lst.append(val)
    return (y_p, y_s) + tuple(jnp.stack(o) for o in outs)
```

```python
import functools

import numpy as np
import jax
import jax.numpy as jnp
from jax import lax
from jax.experimental import pallas as pl
from jax.experimental.pallas import tpu as pltpu

F32 = jnp.float32
BF16 = jnp.bfloat16

D_MODEL = 1024
CHUNK = 64
CHUNK_SHIFT = 6
TOPK_MAX = 256
H_A, HD_A, H_KV = 8, 128, 2
H_IDX, D_IDX = 16, 64
H_G, DK_G, DV_G = 4, 128, 256
GATE_RANK, GATE_TEMP = 16, 16.0
H_M, HD_M = 4, 256
N_GROUPS, E_PER_GROUP = 4, 8
N_EXPERTS = N_GROUPS * E_PER_GROUP
D_EXPERT = 256
EPS = 1e-6
NEG_INF = -1e30
LANES = 128
LOG2E = 1.4426950408889634
GROUP_LANE = LANES - 1
COUNT_ROWS = 128
DSA_TQ = 128
DSA_TK_ATT = 512
MOE_SUB = 128
MOE_E_STEP = 4

QA_OFF, QI_OFF, QG_OFF, KG_OFF, VG_OFF = 0, 1024, 2048, 2560, 3072
GO_OFF, QM_OFF, GB_OFF, KA_OFF, VA_OFF, TAIL_OFF = 4096, 5120, 6144, 9216, 9472, 9728
TAIL_KI, TAIL_WI, TAIL_GLR = 0, 64, 80
C_PAD = 10240
_SRC = dict(qa=(0, 1024), ka=(1024, 1280), va=(1280, 1536), qi=(1536, 2560), ki=(2560, 2624),
            wi=(2624, 2640), qg=(2640, 3152), kg=(3152, 3664), vg=(3664, 4688), glr=(4688, 4704),
            go=(4704, 5728), qm=(5728, 6752), gb=(6752, 9824))
_DST_ORDER = ("qa", "qi", "qg", "kg", "vg", "go", "qm", "gb", "ka", "va", "ki", "wi", "glr")

INT_MIN = -2 ** 31
_KEY_VALID = int(np.float32(NEG_INF * 0.5).view(np.int32)) ^ 0x7FFFFFFF
if _KEY_VALID >= 2 ** 31:
    _KEY_VALID -= 2 ** 32

VMEM_LIMIT = 56 * 1024 * 1024


def _params(*sem):
    return pltpu.CompilerParams(dimension_semantics=sem, vmem_limit_bytes=VMEM_LIMIT)


def _rms(x, g):
    return x * lax.rsqrt(jnp.mean(x * x, axis=-1, keepdims=True) + EPS) * g


def _dot(a, b):
    return jnp.dot(a, b, preferred_element_type=F32)


def _dot_nt(a, b):
    return lax.dot_general(a, b, (((1,), (1,)), ((), ())), preferred_element_type=F32)


def _dot_tn(a, b):
    return lax.dot_general(a, b, (((0,), (0,)), ((), ())), preferred_element_type=F32)


def _norm_proj_kernel(x_ref, g_ref, w_ref, o_ref, h_ref):
    @pl.when(pl.program_id(1) == 0)
    def _():
        h_ref[...] = _rms(x_ref[...], g_ref[...]).astype(BF16)

    o_ref[...] = _dot(h_ref[...], w_ref[...])


def _norm_proj(x, g, w, tm, tn):
    n, d = x.shape
    c = w.shape[1]
    return pl.pallas_call(
        _norm_proj_kernel,
        grid=(n // tm, c // tn),
        in_specs=[pl.BlockSpec((tm, d), lambda i, j: (i, 0)),
                  pl.BlockSpec((1, d), lambda i, j: (0, 0)),
                  pl.BlockSpec((d, tn), lambda i, j: (0, j))],
        out_specs=pl.BlockSpec((tm, tn), lambda i, j: (i, j)),
        out_shape=jax.ShapeDtypeStruct((n, c), F32),
        scratch_shapes=[pltpu.VMEM((tm, d), BF16)],
        compiler_params=_params("parallel", "arbitrary"),
        name="norm_proj",
    )(x, g, w)


def _kv_post_kernel(kv_ref, tail_ref, kg_ref, ig_ref, k_ref, v_ref, ki_ref, *operand_refs):
    kv = kv_ref[...]
    for h in range(H_KV):
        k_ref[:, h * HD_A:(h + 1) * HD_A] = _rms(kv[:, h * HD_A:(h + 1) * HD_A], kg_ref[...])
    v = kv[:, H_KV * HD_A:]
    v_ref[...] = v
    ki = _rms(tail_ref[:, TAIL_KI:TAIL_KI + D_IDX], ig_ref[...])
    ki_ref[...] = ki
    if operand_refs:
        kb_ref, ki2_ref, vt_ref = operand_refs
        kb_ref[...] = k_ref[...].astype(BF16)
        ki2_ref[...] = jnp.concatenate([ki, ki], axis=1).astype(BF16)
        for g in range(H_KV):
            vt_ref[g] = v[:, g * HD_A:(g + 1) * HD_A].T.astype(BF16)


def _kv_post(p, k_g, ik_g, *, b, t, tm, with_operands):
    n = b * t
    nt = t // tm
    w = 2 * H_KV * HD_A
    wk = H_KV * HD_A
    row = lambda bb, i: (bb * nt + i, 0)
    out_specs = [pl.BlockSpec((tm, wk), row), pl.BlockSpec((tm, wk), row), pl.BlockSpec((tm, D_IDX), row)]
    out_shape = [jax.ShapeDtypeStruct((n, wk), F32), jax.ShapeDtypeStruct((n, wk), F32),
                 jax.ShapeDtypeStruct((n, D_IDX), F32)]
    if with_operands:
        out_specs += [pl.BlockSpec((tm, wk), row), pl.BlockSpec((tm, LANES), row),
                      pl.BlockSpec((None, H_KV, HD_A, tm), lambda bb, i: (bb, 0, 0, i))]
        out_shape += [jax.ShapeDtypeStruct((n, wk), BF16), jax.ShapeDtypeStruct((n, LANES), BF16),
                      jax.ShapeDtypeStruct((b, H_KV, HD_A, t), BF16)]
    return pl.pallas_call(
        _kv_post_kernel,
        grid=(b, nt),
        in_specs=[pl.BlockSpec((tm, w), lambda bb, i: (bb * nt + i, KA_OFF // w)),
                  pl.BlockSpec((tm, LANES), lambda bb, i: (bb * nt + i, TAIL_OFF // LANES)),
                  pl.BlockSpec((1, HD_A), lambda bb, i: (0, 0)),
                  pl.BlockSpec((1, D_IDX), lambda bb, i: (0, 0))],
        out_specs=out_specs,
        out_shape=out_shape,
        compiler_params=_params("parallel", "parallel"),
        name="kv_post",
    )(p, p, k_g, ik_g)


def _to_key(x):
    x = jnp.where(x == 0.0, 0.0, x)
    b = lax.bitcast_convert_type(x, jnp.int32)
    return jnp.where(b < 0, b ^ 0x7FFFFFFF, b)


def _dsa_kernel(qa_ref, qi_ref, tail_ref, k_ref, v_ref, ki_ref, qg_ref, o_ref,
                key_ref, m_ref, acc_ref, *, tq, tk, s_valid, past, top_k):
    i = pl.program_id(1)
    q0 = past + i * tq
    lim = jnp.minimum((((q0 + tq - 1) >> CHUNK_SHIFT) + 1) << CHUNK_SHIFT, s_valid)
    n_tiles = (lim + tk - 1) // tk
    lane = lax.broadcasted_iota(jnp.int32, (tq, LANES), 1)
    q_chunk = (q0 + lax.broadcasted_iota(jnp.int32, (tq, 1), 0)) >> CHUNK_SHIFT

    qi = qi_ref[...]
    qi_h = []
    for j in range(H_IDX // 2):
        pair = qi[:, j * LANES:(j + 1) * LANES] * (D_IDX ** -0.5)
        qi_h.append(jnp.where(lane < D_IDX, pair, 0.0).astype(BF16))
        qi_h.append(jnp.where(lane >= D_IDX, pair, 0.0).astype(BF16))
    wi = tail_ref[:, TAIL_WI:TAIL_WI + H_IDX] * (H_IDX ** -0.5)

    def tile_off(j):
        return pl.multiple_of(j * tk, tk)

    def score_tile(j, carry):
        off = tile_off(j)
        kt = ki_ref[:, pl.ds(off, tk)]
        acc = jnp.zeros((tq, tk), F32)
        for h in range(H_IDX):
            acc = acc + jnp.maximum(_dot(qi_h[h], kt), 0.0) * wi[:, h:h + 1]
        kpos = off + lax.broadcasted_iota(jnp.int32, (1, tk), 1)
        adm = ((kpos >> CHUNK_SHIFT) <= q_chunk) & (kpos < s_valid)
        key_ref[:, pl.ds(off, tk)] = _to_key(jnp.where(adm, acc, NEG_INF))
        return carry

    lax.fori_loop(0, n_tiles, score_tile, 0)

    rb = min(tq, COUNT_ROWS)

    def count(cmp, thr):
        outs = []
        for r0 in range(0, tq, rb):
            th = thr[r0:r0 + rb]

            def body(j, c, r0=r0, th=th):
                m = jnp.where(cmp(key_ref[r0:r0 + rb, pl.ds(tile_off(j), tk)], th), 1.0, 0.0)
                part = m[:, 0:LANES]
                for u in range(1, tk // LANES):
                    part = part + m[:, u * LANES:(u + 1) * LANES]
                return c + part

            c = lax.fori_loop(0, n_tiles, body, jnp.zeros((rb, LANES), F32))
            outs.append(jnp.sum(c, axis=-1, keepdims=True))
        return outs[0] if len(outs) == 1 else jnp.concatenate(outs, axis=0)

    ge = lambda a, b: a >= b
    gt = lambda a, b: a > b

    def bit_step(it, v):
        trial = v ^ lax.shift_left(jnp.int32(1), 31 - it)
        return jnp.where(count(ge, trial) >= float(top_k), trial, v)

    v = lax.fori_loop(0, 32, bit_step, jnp.full((tq, 1), INT_MIN, jnp.int32))

    excess = (count(ge, v) > float(top_k)) & (v > _KEY_VALID)

    @pl.when(jnp.max(jnp.where(excess, 1.0, 0.0)) > 0.0)
    def _():
        need = float(top_k) - count(gt, v)
        r_i = lax.broadcasted_iota(jnp.int32, (tk, tk), 0)
        c_i = lax.broadcasted_iota(jnp.int32, (tk, tk), 1)
        before = jnp.where(r_i < c_i, 1.0, 0.0).astype(BF16)

        def body(j, seen):
            off = tile_off(j)
            kt = key_ref[:, pl.ds(off, tk)]
            eq = kt == v
            eqf = jnp.where(eq, 1.0, 0.0)
            rank = seen + _dot(eqf.astype(BF16), before)
            key_ref[:, pl.ds(off, tk)] = jnp.where(eq & (rank >= need), v - 1, kt)
            return seen + jnp.sum(eqf, axis=-1, keepdims=True)

        lax.fori_loop(0, n_tiles, body, jnp.zeros((tq, 1), F32))

    qa = qa_ref[...]
    rep = H_A // H_KV
    qs = []
    for g in range(H_KV):
        heads = [(_rms(qa[:, h * HD_A:(h + 1) * HD_A], qg_ref[...]) * (HD_A ** -0.5 * LOG2E)).astype(BF16)
                 for h in range(g * rep, (g + 1) * rep)]
        qs.append(jnp.concatenate(heads, axis=0))
    m_ref[...] = jnp.full(m_ref.shape, NEG_INF, F32)
    acc_ref[...] = jnp.zeros(acc_ref.shape, F32)

    def att_tile(j, carry):
        off = tile_off(j)
        kt_key = key_ref[:, pl.ds(off, tk)]
        bias = jnp.where((kt_key >= v) & (kt_key > _KEY_VALID), 0.0, NEG_INF)
        logits = [_dot(qs[g], k_ref[g, :, pl.ds(off, tk)]) for g in range(H_KV)]
        for g in range(H_KV):
            vt = v_ref[pl.ds(off, tk), 2 * g * HD_A:2 * (g + 1) * HD_A]
            s = (logits[g].reshape(rep, tq, tk) + bias[None]).reshape(rep * tq, tk)
            m_old = m_ref[g]
            m_new = jnp.maximum(m_old, jnp.max(s, axis=-1, keepdims=True))
            m_sub = jnp.where(m_new > NEG_INF * 0.5, m_new, -NEG_INF)
            p = jnp.concatenate([jnp.exp2(s[:, u * LANES:(u + 1) * LANES] - m_sub)
                                 for u in range(tk // LANES)], axis=1)
            alpha = jnp.exp2(m_old - m_new)
            pv = _dot(p.astype(BF16), vt)
            for u in range(2):
                cols = slice(u * HD_A, (u + 1) * HD_A)
                acc_ref[g, :, cols] = alpha * acc_ref[g, :, cols] + pv[:, cols]
            m_ref[g] = m_new
        return carry

    lax.fori_loop(0, n_tiles, att_tile, 0)
    for g in range(H_KV):
        out = acc_ref[g, :, 0:HD_A] / acc_ref[g, :, HD_A:2 * HD_A]
        for r in range(rep):
            h = g * rep + r
            o_ref[:, h * HD_A:(h + 1) * HD_A] = out[r * tq:(r + 1) * tq].astype(o_ref.dtype)


def _dsa(p, k_all, v_all, ki2, q_g, *, b, t, tq, tk, s_valid, past, top_k):
    nq = t // tq
    s_pad = k_all.shape[-1]
    wq = H_A * HD_A
    kern = functools.partial(_dsa_kernel, tq=tq, tk=tk, s_valid=s_valid, past=past, top_k=top_k)
    return pl.pallas_call(
        kern,
        grid=(b, nq),
        in_specs=[pl.BlockSpec((tq, wq), lambda bb, i: (bb * nq + i, QA_OFF // wq)),
                  pl.BlockSpec((tq, wq), lambda bb, i: (bb * nq + i, QI_OFF // wq)),
                  pl.BlockSpec((tq, LANES), lambda bb, i: (bb * nq + i, TAIL_OFF // LANES)),
                  pl.BlockSpec((None, H_KV, HD_A, s_pad), lambda bb, i: (bb, 0, 0, 0)),
                  pl.BlockSpec((None, s_pad, 2 * H_KV * HD_A), lambda bb, i: (bb, 0, 0)),
                  pl.BlockSpec((None, LANES, s_pad), lambda bb, i: (bb, 0, 0)),
                  pl.BlockSpec((1, HD_A), lambda bb, i: (0, 0))],
        out_specs=pl.BlockSpec((tq, wq), lambda bb, i: (bb * nq + i, 0)),
        out_shape=jax.ShapeDtypeStruct((b * t, wq), BF16),
        scratch_shapes=[pltpu.VMEM((tq, s_pad), jnp.int32),
                        pltpu.VMEM((H_KV, H_A // H_KV * tq, LANES), F32),
                        pltpu.VMEM((H_KV, H_A // H_KV * tq, 2 * HD_A), F32)],
        compiler_params=_params("parallel", "arbitrary"),
        name="dsa",
    )(p, p, p, k_all, v_all, ki2, q_g)


def _fold8(x, op, rows=8):
    parts = [x[r:r + rows] for r in range(0, x.shape[0], rows)]
    while len(parts) > 1:
        nxt = [op(parts[a], parts[a + 1]) for a in range(0, len(parts) - 1, 2)]
        if len(parts) % 2:
            nxt.append(parts[-1])
        parts = nxt
    return parts[0]


def _rowsum8(x):
    return _fold8(x, jnp.add)


def _rowmax8(x):
    return _fold8(x, jnp.maximum)


def _dsa_t_kernel(qa_ref, qi_ref, tail_ref, k_ref, vt_ref, ki_ref, qg_ref, o_ref,
                  key_ref, m_ref, l_ref, acc_ref, *, tq, tk, s_valid, past, top_k):
    i = pl.program_id(1)
    q0 = past + i * tq
    lim = jnp.minimum((((q0 + tq - 1) >> CHUNK_SHIFT) + 1) << CHUNK_SHIFT, s_valid)
    n_tiles = (lim + tk - 1) // tk
    q_chunk = (q0 + lax.broadcasted_iota(jnp.int32, (1, tq), 1)) >> CHUNK_SHIFT
    sub = lax.broadcasted_iota(jnp.int32, (LANES, tq), 0)

    qi = qi_ref[...]
    qi_t = []
    for j in range(H_IDX // 2):
        pair_t = (qi[:, j * LANES:(j + 1) * LANES] * (D_IDX ** -0.5)).T
        qi_t.append(jnp.concatenate([jnp.where(sub < D_IDX, pair_t, 0.0),
                                     jnp.where(sub >= D_IDX, pair_t, 0.0)], axis=1).astype(BF16))
    wi_t = tail_ref[...].T * (H_IDX ** -0.5)

    def tile_off(j):
        return pl.multiple_of(j * tk, tk)

    def score_tile(j, carry):
        off = tile_off(j)
        kt = ki_ref[pl.ds(off, tk), :]
        acc = jnp.zeros((tk, tq), F32)
        for jj in range(H_IDX // 2):
            r = jnp.maximum(_dot(kt, qi_t[jj]), 0.0)
            h = TAIL_WI + 2 * jj
            acc = acc + r[:, 0:tq] * wi_t[h:h + 1, :] + r[:, tq:2 * tq] * wi_t[h + 1:h + 2, :]
        kpos = off + lax.broadcasted_iota(jnp.int32, (tk, 1), 0)
        adm = ((kpos >> CHUNK_SHIFT) <= q_chunk) & (kpos < s_valid)
        key_ref[pl.ds(off, tk), :] = _to_key(jnp.where(adm, acc, NEG_INF))
        return carry

    lax.fori_loop(0, n_tiles, score_tile, 0)

    def count(cmp, thr):
        def body(j, c):
            m = jnp.where(cmp(key_ref[pl.ds(tile_off(j), tk), :], thr), 1.0, 0.0)
            return c + _rowsum8(m)
        c = lax.fori_loop(0, n_tiles, body, jnp.zeros((8, tq), F32))
        return jnp.sum(c, axis=0, keepdims=True)

    ge = lambda a, b: a >= b
    gt = lambda a, b: a > b

    def bit_step(it, carry):
        v, n_ge = carry
        trial = v ^ lax.shift_left(jnp.int32(1), 31 - it)
        n_trial = count(ge, trial)
        keep = n_trial >= float(top_k)
        return jnp.where(keep, trial, v), jnp.where(keep, n_trial, n_ge)

    v, n_ge = lax.fori_loop(0, 32, bit_step,
                            (jnp.full((1, tq), INT_MIN, jnp.int32),
                             jnp.broadcast_to((n_tiles * tk).astype(F32), (1, tq))))

    excess = (n_ge > float(top_k)) & (v > _KEY_VALID)

    @pl.when(jnp.max(jnp.where(excess, 1.0, 0.0)) > 0.0)
    def _():
        need = float(top_k) - count(gt, v)
        r_i = lax.broadcasted_iota(jnp.int32, (tk, tk), 0)
        c_i = lax.broadcasted_iota(jnp.int32, (tk, tk), 1)
        before = jnp.where(c_i < r_i, 1.0, 0.0).astype(BF16)

        def body(j, seen):
            rows = pl.ds(tile_off(j), tk)
            kt = key_ref[rows, :]
            eq = kt == v
            eqf = jnp.where(eq, 1.0, 0.0)
            rank = seen + _dot(before, eqf.astype(BF16))
            key_ref[rows, :] = jnp.where(eq & (rank >= need), v - 1, kt)
            return seen + jnp.sum(_rowsum8(eqf), axis=0, keepdims=True)

        lax.fori_loop(0, n_tiles, body, jnp.zeros((1, tq), F32))

    rep = H_A // H_KV
    qa = qa_ref[...]
    q_t = []
    for g in range(H_KV):
        heads = [(_rms(qa[:, h * HD_A:(h + 1) * HD_A], qg_ref[...]) * (HD_A ** -0.5 * LOG2E)).T
                 for h in range(g * rep, (g + 1) * rep)]
        q_t.append(jnp.concatenate(heads, axis=1).astype(BF16))
    m_ref[...] = jnp.full(m_ref.shape, NEG_INF, F32)
    l_ref[...] = jnp.zeros(l_ref.shape, F32)
    acc_ref[...] = jnp.zeros(acc_ref.shape, F32)

    half = rep // 2
    ta = min(DSA_TK_ATT, tk)
    n_att = (lim + ta - 1) // ta

    def att_tile(j, carry):
        off = pl.multiple_of(j * ta, ta)
        rows = pl.ds(off, ta)
        kt_key = key_ref[rows, :]
        bias = jnp.where((kt_key >= v) & (kt_key > _KEY_VALID), 0.0, NEG_INF)
        logits = [_dot(k_ref[rows, g * HD_A:(g + 1) * HD_A], q_t[g]) for g in range(H_KV)]
        for g in range(H_KV):
            vt = vt_ref[g, :, rows]
            for r0 in range(0, rep, half):
                blk = slice(r0 * tq, (r0 + half) * tq)
                m_old = m_ref[g, :, blk]
                ps, ms, ls = [], [], []
                for r in range(r0, r0 + half):
                    s = logits[g][:, r * tq:(r + 1) * tq] + bias
                    m_o = m_old[:, (r - r0) * tq:(r - r0 + 1) * tq]
                    m_n = jnp.maximum(m_o, jnp.max(_rowmax8(s), axis=0, keepdims=True))
                    p = jnp.exp2(s - jnp.where(m_n > NEG_INF * 0.5, m_n, -NEG_INF)[0:1])
                    ps.append(p.astype(BF16))
                    ms.append(m_n)
                    ls.append(jnp.sum(_rowsum8(p), axis=0, keepdims=True))
                m_new = jnp.concatenate(ms, axis=1)
                alpha = jnp.exp2(m_old - m_new)
                l_ref[g, :, blk] = alpha * l_ref[g, :, blk] + jnp.concatenate(ls, axis=1)
                pv = _dot(vt, jnp.concatenate(ps, axis=1))
                acc_ref[g, :, blk] = alpha[0:1] * acc_ref[g, :, blk] + pv
                m_ref[g, :, blk] = m_new
        return carry

    lax.fori_loop(0, n_att, att_tile, 0)
    for g in range(H_KV):
        out = acc_ref[g] / l_ref[g][0:1]
        for r in range(rep):
            h = g * rep + r
            o_ref[:, h * HD_A:(h + 1) * HD_A] = out[:, r * tq:(r + 1) * tq].T.astype(o_ref.dtype)


def _dsa_t(p, k_all, vt_all, ki2, q_g, *, b, t, tq, tk, s_valid, past, top_k):
    nq = t // tq
    s_pad = k_all.shape[1]
    wq = H_A * HD_A
    rep = H_A // H_KV
    kern = functools.partial(_dsa_t_kernel, tq=tq, tk=tk, s_valid=s_valid, past=past, top_k=top_k)
    return pl.pallas_call(
        kern,
        grid=(b, nq),
        in_specs=[pl.BlockSpec((tq, wq), lambda bb, i: (bb * nq + i, QA_OFF // wq)),
                  pl.BlockSpec((tq, wq), lambda bb, i: (bb * nq + i, QI_OFF // wq)),
                  pl.BlockSpec((tq, LANES), lambda bb, i: (bb * nq + i, TAIL_OFF // LANES)),
                  pl.BlockSpec((None, s_pad, H_KV * HD_A), lambda bb, i: (bb, 0, 0)),
                  pl.BlockSpec((None, H_KV, HD_A, s_pad), lambda bb, i: (bb, 0, 0, 0)),
                  pl.BlockSpec((None, s_pad, LANES), lambda bb, i: (bb, 0, 0)),
                  pl.BlockSpec((1, HD_A), lambda bb, i: (0, 0))],
        out_specs=pl.BlockSpec((tq, wq), lambda bb, i: (bb * nq + i, 0)),
        out_shape=jax.ShapeDtypeStruct((b * t, wq), BF16),
        scratch_shapes=[pltpu.VMEM((s_pad, tq), jnp.int32),
                        pltpu.VMEM((H_KV, 8, rep * tq), F32),
                        pltpu.VMEM((H_KV, 8, rep * tq), F32),
                        pltpu.VMEM((H_KV, HD_A, rep * tq), F32)],
        compiler_params=_params("parallel", "arbitrary"),
        name="dsa_t",
    )(p, p, p, k_all, vt_all, ki2, q_g)


def _gla_kernel(q_ref, k_ref, v_ref, go_ref, tail_ref, wup_ref, bup_ref, ng_ref, s0_ref,
                o_ref, s_ref, la_ref, *, tt, chunk):
    @pl.when(pl.program_id(1) == 0)
    def _():
        s_ref[...] = s0_ref[...]

    glr = tail_ref[:, TAIL_GLR:TAIL_GLR + GATE_RANK].astype(BF16)
    z = _dot(glr, wup_ref[...]) + bup_ref[...]
    la_ref[...] = (jnp.minimum(z, 0.0) - jnp.log1p(jnp.exp(-jnp.abs(z)))) / GATE_TEMP

    width = H_G * DK_G
    row = lax.broadcasted_iota(jnp.int32, (chunk, width), 0)
    causal = (lax.broadcasted_iota(jnp.int32, (chunk, chunk), 0)
              >= lax.broadcasted_iota(jnp.int32, (chunk, chunk), 1))
    eye = (lax.broadcasted_iota(jnp.int32, (DK_G, DK_G), 0)
           == lax.broadcasted_iota(jnp.int32, (DK_G, DK_G), 1))

    def body(c, carry):
        off = pl.multiple_of(c * chunk, chunk)
        rows = pl.ds(off, chunk)
        bcum = la_ref[rows, :]
        step = 1
        while step < chunk:
            bcum = bcum + jnp.where(row >= step, pltpu.roll(bcum, step, axis=0), 0.0)
            step *= 2
        b_last = bcum[chunk - 1:chunk, :]
        q = q_ref[rows, :] * (DK_G ** -0.5)
        k = k_ref[rows, :]
        qe = (q * jnp.exp(bcum)).astype(BF16)
        ke = (k * jnp.exp(-bcum)).astype(BF16)
        kd = (k * jnp.exp(b_last - bcum)).astype(BF16)
        decay = jnp.exp(b_last)
        hk = [slice(h * DK_G, (h + 1) * DK_G) for h in range(H_G)]
        hv = [slice(h * DV_G, (h + 1) * DV_G) for h in range(H_G)]
        att = [jnp.where(causal, _dot_nt(qe[:, hk[h]], ke[:, hk[h]]), 0.0).astype(BF16)
               for h in range(H_G)]
        vv = [v_ref[rows, hv[h]].astype(BF16) for h in range(H_G)]
        state = [s_ref[h] for h in range(H_G)]
        inter = [_dot(qe[:, hk[h]], state[h].astype(BF16)) for h in range(H_G)]
        intra = [_dot(att[h], vv[h]) for h in range(H_G)]
        upd = [_dot_tn(kd[:, hk[h]], vv[h]) for h in range(H_G)]
        for h in range(H_G):
            decay_col = jnp.sum(jnp.where(eye, jnp.broadcast_to(decay[:, hk[h]], (DK_G, DK_G)), 0.0),
                                axis=1, keepdims=True)
            s_ref[h] = state[h] * decay_col + upd[h]
            go = go_ref[rows, hv[h]]
            o_ref[rows, hv[h]] = (_rms(intra[h] + inter[h], ng_ref[...])
                                  * (go * jax.nn.sigmoid(go))).astype(o_ref.dtype)
        return carry

    n_chunks = tt // chunk
    lax.fori_loop(0, n_chunks, body, 0, unroll=2 if n_chunks % 2 == 0 else 1)


def _gla(p, w_up, b_up, n_g, s0, *, b, t, tt, chunk):
    nt = t // tt
    kern = functools.partial(_gla_kernel, tt=tt, chunk=chunk)
    wk, wv = H_G * DK_G, H_G * DV_G
    rowblk = lambda w, off: pl.BlockSpec((tt, w), lambda bb, i: (bb * nt + i, off // w))
    state = pl.BlockSpec((None, H_G, DK_G, DV_G), lambda bb, i: (bb, 0, 0, 0))
    return pl.pallas_call(
        kern,
        grid=(b, nt),
        in_specs=[rowblk(wk, QG_OFF), rowblk(wk, KG_OFF), rowblk(wv, VG_OFF), rowblk(wv, GO_OFF),
                  rowblk(LANES, TAIL_OFF),
                  pl.BlockSpec((GATE_RANK, wk), lambda bb, i: (0, 0)),
                  pl.BlockSpec((1, wk), lambda bb, i: (0, 0)),
                  pl.BlockSpec((1, DV_G), lambda bb, i: (0, 0)),
                  state],
        out_specs=[pl.BlockSpec((tt, wv), lambda bb, i: (bb * nt + i, 0)), state],
        out_shape=[jax.ShapeDtypeStruct((b * t, wv), BF16),
                   jax.ShapeDtypeStruct((b, H_G, DK_G, DV_G), F32)],
        scratch_shapes=[pltpu.VMEM((tt, wk), F32)],
        compiler_params=_params("parallel", "arbitrary"),
        name="gla",
    )(p, p, p, p, p, w_up, b_up, n_g, s0)


def _mem_kv_kernel(x_ref, g_ref, w_ref, kg_ref, mk_ref, mv_ref):
    kv = _dot(_rms(x_ref[...], g_ref[...]).astype(BF16), w_ref[...])
    for h in range(H_M):
        mk_ref[:, h * HD_M:(h + 1) * HD_M] = _rms(kv[:, h * HD_M:(h + 1) * HD_M], kg_ref[...])
    mv_ref[...] = kv[:, H_M * HD_M:]


def _mem_kv(mem, g, w, k_g, tm):
    n, d = mem.shape
    wk = H_M * HD_M
    return pl.pallas_call(
        _mem_kv_kernel,
        grid=(n // tm,),
        in_specs=[pl.BlockSpec((tm, d), lambda i: (i, 0)),
                  pl.BlockSpec((1, d), lambda i: (0, 0)),
                  pl.BlockSpec((d, 2 * wk), lambda i: (0, 0)),
                  pl.BlockSpec((1, HD_M), lambda i: (0, 0))],
        out_specs=[pl.BlockSpec((tm, wk), lambda i: (i, 0)),
                   pl.BlockSpec((tm, wk), lambda i: (i, 0))],
        out_shape=[jax.ShapeDtypeStruct((n, wk), F32), jax.ShapeDtypeStruct((n, wk), F32)],
        compiler_params=_params("parallel"),
        name="mem_kv",
    )(mem, g, w, k_g)


def _mem_attn_kernel(q_ref, mk_ref, mv_ref, g_ref, o_ref):
    qm = q_ref[...]
    for h in range(H_M):
        cols = slice(h * HD_M, (h + 1) * HD_M)
        q = (_rms(qm[:, cols], g_ref[...]) * (HD_M ** -0.5)).astype(BF16)
        s = _dot_nt(q, mk_ref[:, cols])
        e = jnp.exp(s - jnp.max(s, axis=-1, keepdims=True))
        prob = e / jnp.sum(e, axis=-1, keepdims=True)
        o_ref[:, cols] = _dot(prob.astype(BF16), mv_ref[:, cols]).astype(o_ref.dtype)


def _mem_attn(p, mk, mv, q_g, *, b, t, tt):
    nt = t // tt
    wq = H_M * HD_M
    n_mem = mk.shape[1]
    return pl.pallas_call(
        _mem_attn_kernel,
        grid=(b, nt),
        in_specs=[pl.BlockSpec((tt, wq), lambda bb, i: (bb * nt + i, QM_OFF // wq)),
                  pl.BlockSpec((None, n_mem, wq), lambda bb, i: (bb, 0, 0)),
                  pl.BlockSpec((None, n_mem, wq), lambda bb, i: (bb, 0, 0)),
                  pl.BlockSpec((1, HD_M), lambda bb, i: (0, 0))],
        out_specs=pl.BlockSpec((tt, wq), lambda bb, i: (bb * nt + i, 0)),
        out_shape=jax.ShapeDtypeStruct((b * t, wq), BF16),
        compiler_params=_params("parallel", "arbitrary"),
        name="mem_attn",
    )(p, mk, mv, q_g)


def _merge_kernel(x_ref, oa_ref, og_ref, om_ref, gb_ref, wa_ref, wg_ref, wm_ref, wo_ref,
                  fg_ref, wr_ref, br_ref, x1_ref, h2_ref, cb_ref):
    d = D_MODEL
    merged = (jax.nn.sigmoid(gb_ref[:, 0:d]) * _dot(oa_ref[...], wa_ref[...])
              + jax.nn.sigmoid(gb_ref[:, d:2 * d]) * _dot(og_ref[...], wg_ref[...])
              + jax.nn.sigmoid(gb_ref[:, 2 * d:3 * d]) * _dot(om_ref[...], wm_ref[...]))
    x1 = x_ref[...] + _dot(merged.astype(BF16), wo_ref[...])
    x1_ref[...] = x1
    hb = _rms(x1, fg_ref[...]).astype(BF16)
    h2_ref[...] = hb

    lg = _dot(hb, wr_ref[...]) + br_ref[...]
    lane = lax.broadcasted_iota(jnp.int32, lg.shape, 1).astype(F32)
    big = 1e9
    gmask = (lane >= N_EXPERTS) & (lane < N_EXPERTS + N_GROUPS)
    gl = jnp.where(gmask, lg, NEG_INF)
    gmax = jnp.max(gl, axis=-1, keepdims=True)
    gidx = jnp.min(jnp.where(gmask & (gl == gmax), lane, big), axis=-1, keepdims=True) - N_EXPERTS
    g_w = 1.0 / jnp.sum(jnp.where(gmask, jnp.exp(gl - gmax), 0.0), axis=-1, keepdims=True)
    lo = gidx * E_PER_GROUP
    emask = (lane >= lo) & (lane < lo + E_PER_GROUP)
    el = jnp.where(emask, lg, NEG_INF)
    ee = jnp.where(emask, jnp.exp(el - jnp.max(el, axis=-1, keepdims=True)), 0.0)
    ep = ee / jnp.sum(ee, axis=-1, keepdims=True)
    p1 = jnp.max(jnp.where(emask, ep, -1.0), axis=-1, keepdims=True)
    i1 = jnp.min(jnp.where(emask & (ep == p1), lane, big), axis=-1, keepdims=True)
    rest = emask & (lane != i1)
    p2 = jnp.max(jnp.where(rest, ep, -1.0), axis=-1, keepdims=True)
    i2 = jnp.min(jnp.where(rest & (ep == p2), lane, big), axis=-1, keepdims=True)
    den = p1 + p2
    cb_ref[...] = (jnp.where(lane == i1, g_w * (p1 / den), 0.0)
                   + jnp.where(lane == i2, g_w * (p2 / den), 0.0)
                   + jnp.where(lane == GROUP_LANE, gidx, 0.0))


def _merge(x, oa, og, om, p, wa, wg, wm, wo, f_g, w_r, b_r, tm):
    n, d = x.shape
    full = lambda shape: pl.BlockSpec(shape, lambda i: (0, 0))
    rowblk = lambda w: pl.BlockSpec((tm, w), lambda i: (i, 0))
    return pl.pallas_call(
        _merge_kernel,
        grid=(n // tm,),
        in_specs=[rowblk(d), rowblk(d), rowblk(d), rowblk(d),
                  pl.BlockSpec((tm, 3 * d), lambda i: (i, GB_OFF // (3 * d))),
                  full((d, d)), full((d, d)), full((d, d)), full((d, d)),
                  full((1, d)), full((d, LANES)), full((1, LANES))],
        out_specs=[rowblk(d), rowblk(d), rowblk(LANES)],
        out_shape=[jax.ShapeDtypeStruct((n, d), F32),
                   jax.ShapeDtypeStruct((n, d), BF16),
                   jax.ShapeDtypeStruct((n, LANES), F32)],
        compiler_params=_params("parallel"),
        name="merge",
    )(x, oa, og, om, p, wa, wg, wm, wo, f_g, w_r, b_r)


def _split3(x):
    hi = x.astype(BF16)
    r1 = x - hi.astype(F32)
    mid = r1.astype(BF16)
    lo = (r1 - mid.astype(F32)).astype(BF16)
    return hi, mid, lo


def _permute_rows(onehot, x):
    hi, mid, lo = _split3(x)
    return _dot(onehot, hi) + _dot(onehot, mid) + _dot(onehot, lo)


def _moe_kernel(h_ref, cb_ref, x1_ref, wg_ref, wu_ref, wd_ref, y_ref,
                hp_ref, cbp_ref, yp_ref, dest_ref, seg_ref):
    e = pl.program_id(1)
    tm = h_ref.shape[0]

    @pl.when(e == 0)
    def _():
        cb = cb_ref[...]
        g_row = cb.T[GROUP_LANE:GROUP_LANE + 1, :]
        grp = lax.broadcasted_iota(jnp.int32, (8, tm), 0).astype(F32)
        member = jnp.where(g_row == grp, 1.0, 0.0)
        r_i = lax.broadcasted_iota(jnp.int32, (tm, tm), 0)
        c_i = lax.broadcasted_iota(jnp.int32, (tm, tm), 1)
        upto = jnp.where(r_i <= c_i, 1.0, 0.0).astype(BF16)
        cnt = _dot(member.astype(BF16), upto)
        tot = cnt[:, tm - 1:tm]
        start = jnp.zeros((1, 1), F32)
        dest = jnp.zeros((1, tm), F32)
        seg_ref[0] = 0
        for j in range(N_GROUPS):
            dest = dest + member[j:j + 1] * (start + cnt[j:j + 1] - 1.0)
            start = start + tot[j:j + 1]
            seg_ref[j + 1] = jnp.sum(start).astype(jnp.int32)
        place = jnp.where(r_i.astype(F32) == dest, 1.0, 0.0).astype(BF16)
        hp_ref[...] = _dot(place, h_ref[...]).astype(BF16)
        cbp_ref[...] = _permute_rows(place, cb)
        dest_ref[...] = jnp.broadcast_to(dest, (LANES, tm)).T
        yp_ref[...] = jnp.zeros(yp_ref.shape, F32)

    sub = min(MOE_SUB, tm)
    n_e = wg_ref.shape[0]
    g = (e * n_e) // E_PER_GROUP
    first = seg_ref[g] // sub
    last = (seg_ref[g + 1] + sub - 1) // sub
    lane = lax.broadcasted_iota(jnp.int32, (sub, LANES), 1)

    def sub_tile(s, carry):
        rows = pl.ds(pl.multiple_of(s * sub, sub), sub)
        h = hp_ref[rows, :]
        cbp = cbp_ref[rows, :]
        gate = [_dot(h, wg_ref[k]) for k in range(n_e)]
        up = [_dot(h, wu_ref[k]) for k in range(n_e)]
        out = None
        for k in range(n_e):
            c = jnp.sum(jnp.where(lane == e * n_e + k, cbp, 0.0), axis=-1, keepdims=True)
            hid = (gate[k] * jax.nn.sigmoid(gate[k])) * up[k] * c
            part = _dot(hid.astype(BF16), wd_ref[k])
            out = part if out is None else out + part
        yp_ref[rows, :] += out
        return carry

    lax.fori_loop(first, last, sub_tile, 0)

    @pl.when(e == pl.num_programs(1) - 1)
    def _():
        col = lax.broadcasted_iota(jnp.int32, (tm, tm), 1).astype(F32)
        back = jnp.where(col == dest_ref[:, 0:1], 1.0, 0.0).astype(BF16)
        y_ref[...] = x1_ref[...] + _permute_rows(back, yp_ref[...])


def _moe(h2, cb, x1, wg, wu, wd, tm):
    n, d = x1.shape
    return pl.pallas_call(
        _moe_kernel,
        grid=(n // tm, N_EXPERTS // MOE_E_STEP),
        in_specs=[pl.BlockSpec((tm, d), lambda i, e: (i, 0)),
                  pl.BlockSpec((tm, LANES), lambda i, e: (i, 0)),
                  pl.BlockSpec((tm, d), lambda i, e: (i, 0)),
                  pl.BlockSpec((MOE_E_STEP, d, D_EXPERT), lambda i, e: (e, 0, 0)),
                  pl.BlockSpec((MOE_E_STEP, d, D_EXPERT), lambda i, e: (e, 0, 0)),
                  pl.BlockSpec((MOE_E_STEP, D_EXPERT, d), lambda i, e: (e, 0, 0))],
        out_specs=pl.BlockSpec((tm, d), lambda i, e: (i, 0)),
        out_shape=jax.ShapeDtypeStruct((n, d), F32),
        scratch_shapes=[pltpu.VMEM((tm, d), BF16),
                        pltpu.VMEM((tm, LANES), F32),
                        pltpu.VMEM((tm, d), F32),
                        pltpu.VMEM((tm, LANES), F32),
                        pltpu.SMEM((8,), jnp.int32)],
        compiler_params=_params("parallel", "arbitrary"),
        name="moe",
    )(h2, cb, x1, wg, wu, wd)


def _round_up(n, m):
    return (n + m - 1) // m * m


def _layer(x, mk, mv, past_k, past_v, past_ki, s0, w):
    b, t, d = x.shape
    n = b * t
    x2 = x.reshape(n, d)
    p = _norm_proj(x2, w["attn_g"], w["w_cat"], min(1024, n), 2048)
    tk = 512
    past = 0 if past_k is None else past_k.shape[1]
    dsa_args = dict(b=b, t=t, tk=tk, s_valid=past + t, past=past, top_k=min(TOPK_MAX, (past + t) // 4))
    direct = past_k is None and t % tk == 0
    post = _kv_post(p, w["k_g"], w["ik_g"], b=b, t=t, tm=min(512, t), with_operands=direct)
    ka, va, ki = post[:3]
    if direct:
        k_bf, ki2_bf, vt_all = post[3:]
        oa = _dsa_t(p, k_bf.reshape(b, t, -1), vt_all, ki2_bf.reshape(b, t, -1), w["q_g"],
                    tq=DSA_TQ, **dsa_args)
    else:
        oa = _dsa_from_rows(p, ka, va, ki, past_k, past_v, past_ki, w["q_g"], dsa_args)

    chunk = CHUNK if t % CHUNK == 0 else t
    tt = min(512, t)
    og, s_new = _gla(p, w["w_up"], w["b_up"], w["gla_g"], s0, b=b, t=t, tt=tt, chunk=chunk)
    om = _mem_attn(p, mk, mv, w["mq_g"], b=b, t=t, tt=tt)
    x1, h2, cb = _merge(x2, oa, og, om, p, w["w_a"], w["w_g"], w["w_m"], w["w_o"],
                        w["ffn_g"], w["w_r"], w["b_r"], min(512, n))
    y = _moe(h2, cb, x1, w["w_eg"], w["w_eu"], w["w_ed"], min(1024, n))
    return (y.reshape(b, t, d), ka.reshape(b, t, H_KV, HD_A), va.reshape(b, t, H_KV, HD_A),
            ki.reshape(b, t, D_IDX), s_new)


def _dsa_from_rows(p, ka, va, ki, past_k, past_v, past_ki, q_g, dsa_args):
    b, t, tk, s_valid = dsa_args["b"], dsa_args["t"], dsa_args["tk"], dsa_args["s_valid"]
    k3 = ka.astype(BF16).reshape(b, t, H_KV * HD_A)
    v3 = va.astype(BF16).reshape(b, t, H_KV * HD_A)
    ki3 = ki.astype(BF16).reshape(b, t, D_IDX)
    if past_k is not None:
        past = past_k.shape[1]
        k3 = jnp.concatenate([past_k.astype(BF16).reshape(b, past, -1), k3], axis=1)
        v3 = jnp.concatenate([past_v.astype(BF16).reshape(b, past, -1), v3], axis=1)
        ki3 = jnp.concatenate([past_ki.astype(BF16), ki3], axis=1)
    pad = ((0, 0), (0, _round_up(s_valid, tk) - s_valid), (0, 0))
    k_rows = jnp.pad(k3, pad)
    ki2 = jnp.pad(jnp.concatenate([ki3, ki3], axis=-1), pad)
    if t % DSA_TQ == 0:
        vt_all = jnp.pad(v3, pad).reshape(b, -1, H_KV, HD_A).transpose(0, 2, 3, 1)
        return _dsa_t(p, k_rows, vt_all, ki2, q_g, tq=DSA_TQ, **dsa_args)
    k_all = k_rows.reshape(b, -1, H_KV, HD_A).transpose(0, 2, 3, 1)
    ones = jnp.ones((b, s_valid, HD_A), BF16)
    v_parts = []
    for g in range(H_KV):
        v_parts += [v3[..., g * HD_A:(g + 1) * HD_A], ones]
    v_all = jnp.pad(jnp.concatenate(v_parts, axis=-1), pad)
    return _dsa(p, k_all, v_all, ki2.transpose(0, 2, 1), q_g, tq=t, **dsa_args)


def _prep_weights(l, attn_norm_g, w_in, q_norm_g, k_norm_g, idx_k_norm_g, w_gla_gate_up,
                  b_gla_gate_up, gla_out_norm_g, mem_q_norm_g, w_branch_a, w_branch_g,
                  w_branch_m, w_out, ffn_norm_g, w_group_router, b_group_router,
                  w_expert_router, b_expert_router, w_exp_gate, w_exp_up, w_exp_down):
    d = D_MODEL
    wi = w_in[l]
    cols = [wi[:, _SRC[name][0]:_SRC[name][1]] for name in _DST_ORDER]
    used = sum(c.shape[1] for c in cols)
    w_cat = jnp.concatenate(cols + [jnp.zeros((d, C_PAD - used), F32)], axis=1).astype(BF16)
    pad_r = LANES - N_EXPERTS - N_GROUPS
    w_r = jnp.concatenate([w_expert_router[l], w_group_router[l], jnp.zeros((d, pad_r), F32)], axis=1)
    b_r = jnp.concatenate([b_expert_router[l], b_group_router[l], jnp.zeros((pad_r,), F32)])
    return dict(
        attn_g=attn_norm_g[l][None], w_cat=w_cat, q_g=q_norm_g[l][None], k_g=k_norm_g[l][None],
        ik_g=idx_k_norm_g[l][None],
        w_up=w_gla_gate_up[l].astype(BF16), b_up=b_gla_gate_up[l][None],
        gla_g=gla_out_norm_g[l][None],
        mq_g=mem_q_norm_g[l][None],
        w_a=w_branch_a[l].astype(BF16), w_g=w_branch_g[l].astype(BF16),
        w_m=w_branch_m[l].astype(BF16), w_o=w_out[l].astype(BF16),
        ffn_g=ffn_norm_g[l][None], w_r=w_r.astype(BF16), b_r=b_r[None],
        w_eg=w_exp_gate[l].astype(BF16), w_eu=w_exp_up[l].astype(BF16),
        w_ed=w_exp_down[l].astype(BF16))


def kernel(x_prompt, x_sample, mem_prompt, cache_k, cache_v, cache_idx_k, state_gla, cache_mem_k, cache_mem_v, attn_norm_g, w_in, q_norm_g, k_norm_g, idx_k_norm_g, w_gla_gate_up, b_gla_gate_up, gla_out_norm_g, mem_norm_g, w_mem_kv, mem_q_norm_g, mem_k_norm_g, w_branch_a, w_branch_g, w_branch_m, w_out, ffn_norm_g, w_group_router, b_group_router, w_expert_router, b_expert_router, w_exp_gate, w_exp_up, w_exp_down):
    depth = w_in.shape[0]
    y_p, y_s = x_prompt, x_sample
    outs = [[] for _ in range(10)]
    for l in range(depth):
        w = _prep_weights(l, attn_norm_g, w_in, q_norm_g, k_norm_g, idx_k_norm_g, w_gla_gate_up,
                          b_gla_gate_up, gla_out_norm_g, mem_q_norm_g, w_branch_a, w_branch_g,
                          w_branch_m, w_out, ffn_norm_g, w_group_router, b_group_router,
                          w_expert_router, b_expert_router, w_exp_gate, w_exp_up, w_exp_down)
        bp, n_mem, d = mem_prompt.shape
        mk_p, mv_p = _mem_kv(mem_prompt.reshape(bp * n_mem, d), mem_norm_g[l][None],
                             w_mem_kv[l].astype(BF16), mem_k_norm_g[l][None], min(256, bp * n_mem))
        mk_p = mk_p.reshape(bp, n_mem, H_M, HD_M)
        mv_p = mv_p.reshape(bp, n_mem, H_M, HD_M)
        s0 = jnp.zeros((bp, H_G, DK_G, DV_G), F32)
        y_p, k_new, v_new, ki_new, s_new = _layer(
            y_p, mk_p.reshape(bp, n_mem, -1).astype(BF16), mv_p.reshape(bp, n_mem, -1).astype(BF16),
            None, None, None, s0, w)
        for lst, val in zip(outs[:6], (k_new, v_new, ki_new, s_new, mk_p, mv_p)):
            lst.append(val)
        bs = x_sample.shape[0]
        y_s, k_new, v_new, ki_new, s_new = _layer(
            y_s, cache_mem_k[l].astype(BF16).reshape(bs, n_mem, -1),
            cache_mem_v[l].astype(BF16).reshape(bs, n_mem, -1),
            cache_k[l], cache_v[l], cache_idx_k[l], state_gla[l], w)
        for lst, val in zip(outs[6:], (k_new, v_new, ki_new, s_new)):
            lst.append(val)
    return (y_p, y_s) + tuple(jnp.stack(o) for o in outs)
```
